```python
import jax, jax.numpy as jnp
from jax import lax
import numpy as np

D_MODEL = 2048
BATCH = 1
SEQ = 8192
DEPTH = 2

N_MIXERS = 2
N_RET_LAYERS = (DEPTH + 1) // 2
N_ATT_LAYERS = DEPTH // 2

RET_HEADS = 8
RET_QK_DIM = D_MODEL // RET_HEADS
RET_V_DIM = 2 * D_MODEL // RET_HEADS
RET_CHUNK = 128
RET_IN_DIM = 2 * RET_HEADS * RET_QK_DIM + 2 * RET_HEADS * RET_V_DIM

ATT_HEAD_DIM = 64
ATT_HEADS = D_MODEL // ATT_HEAD_DIM
ATT_KV_HEADS = ATT_HEADS // 8
ATT_GROUP = ATT_HEADS // ATT_KV_HEADS
WINDOW = 128
ATT_BLOCK = 128
ATT_IN_DIM = (ATT_HEADS + 2 * ATT_KV_HEADS) * ATT_HEAD_DIM

D_FF = ((8 * D_MODEL // 3 + 127) // 128) * 128
CONV_WIDTH = 3

EPS = 1e-6

kernel_name = "hybrid_retention_swa_sink_convffn"


def rmsnorm(x, g):
    xf = x.astype(jnp.float32)
    y = xf * lax.rsqrt(jnp.mean(xf * xf, axis=-1, keepdims=True) + EPS)
    return (y * g.astype(jnp.float32)).astype(x.dtype)


def retention(h, w_in, w_out):
    B, S, _ = h.shape
    H, dk, dv, C = RET_HEADS, RET_QK_DIM, RET_V_DIM, RET_CHUNK
    nc = S // C
    proj = h @ w_in
    q, k, v, g = jnp.split(proj, [H * dk, 2 * H * dk, 2 * H * dk + H * dv], axis=-1)

    def to_chunks(t, d):
        return t.reshape(B, nc, C, H, d).transpose(1, 0, 3, 2, 4)

    q = to_chunks(q, dk)
    k = to_chunks(k, dk) * (dk ** -0.5)
    v = to_chunks(v, dv)

    log_gamma = jnp.log1p(-(2.0 ** (-5.0 - jnp.arange(H, dtype=jnp.float32))))
    n = jnp.arange(C, dtype=jnp.float32)
    diff = n[:, None] - n[None, :]
    intra = jnp.where(diff >= 0, jnp.exp(log_gamma[:, None, None] * jnp.maximum(diff, 0.0)), 0.0)
    q_decay = jnp.exp(log_gamma[:, None] * (n + 1.0))[..., None]
    k_decay = jnp.exp(log_gamma[:, None] * (C - 1.0 - n))[..., None]
    chunk_decay = jnp.exp(log_gamma * C)[:, None, None]

    def step(state, qkv):
        qc, kc, vc = qkv
        scores = jnp.einsum('bhnd,bhmd->bhnm', qc, kc) * intra
        o = (jnp.einsum('bhnm,bhme->bhne', scores, vc)
             + jnp.einsum('bhnd,bhde->bhne', qc * q_decay, state))
        state = state * chunk_decay + jnp.einsum('bhmd,bhme->bhde', kc * k_decay, vc)
        return state, o

    state0 = jnp.zeros((B, H, dk, dv), jnp.float32)
    _, o = lax.scan(step, state0, (q, k, v))
    o = o.transpose(1, 0, 3, 2, 4).reshape(B, S, H, dv).astype(jnp.float32)
    o = o * lax.rsqrt(jnp.mean(o * o, axis=-1, keepdims=True) + EPS)
    y = jax.nn.silu(g.astype(jnp.float32)) * o.reshape(B, S, H * dv)
    return y.astype(h.dtype) @ w_out


def sliding_window_attention(h, w_qkv, b_qkv, sinks, w_out):
    B, S, _ = h.shape
    Hq, Hkv, G, dh, BLK = ATT_HEADS, ATT_KV_HEADS, ATT_GROUP, ATT_HEAD_DIM, ATT_BLOCK
    nb = S // BLK
    proj = h @ w_qkv + b_qkv
    q, k, v = jnp.split(proj, [Hq * dh, (Hq + Hkv) * dh], axis=-1)
    q = q.reshape(B, nb, BLK, Hkv, G, dh)
    k = k.reshape(B, nb, BLK, Hkv, dh)
    v = v.reshape(B, nb, BLK, Hkv, dh)
    pad = ((0, 0), (1, 0), (0, 0), (0, 0), (0, 0))
    kb = jnp.concatenate([jnp.pad(k, pad)[:, :-1], k], axis=2)
    vb = jnp.concatenate([jnp.pad(v, pad)[:, :-1], v], axis=2)

    scores = jnp.einsum('bnqhgd,bnkhd->bnhgqk', q, kb).astype(jnp.float32) * (dh ** -0.5)
    qpos = jnp.arange(BLK) + BLK
    kpos = jnp.arange(2 * BLK)
    dist = qpos[:, None] - kpos[None, :]
    in_window = (dist >= 0) & (dist < WINDOW)
    blk = jnp.arange(nb)
    valid = in_window[None] & ((blk[:, None, None] > 0) | (kpos[None, None, :] >= BLK))
    slopes = (2.0 ** (-8.0 * jnp.arange(1, Hq + 1, dtype=jnp.float32) / Hq)).reshape(Hkv, G)
    scores = scores - slopes[:, :, None, None] * dist.astype(jnp.float32)
    scores = jnp.where(valid[None, :, None, None], scores, -jnp.inf)
    sink = jnp.broadcast_to(sinks.astype(jnp.float32).reshape(Hkv, G)[None, None, :, :, None, None],
                            scores.shape[:-1] + (1,))
    probs = jax.nn.softmax(jnp.concatenate([scores, sink], axis=-1), axis=-1)[..., :-1]
    out = jnp.einsum('bnhgqk,bnkhd->bnqhgd', probs.astype(vb.dtype), vb)
    return out.reshape(B, S, Hq * dh) @ w_out


def conv_ffn(h, w_up, conv_w, conv_b, w_down):
    S = h.shape[1]
    u = h @ w_up
    up = jnp.pad(u, ((0, 0), (CONV_WIDTH - 1, 0), (0, 0)))
    c = conv_b + sum(up[:, j:j + S] * conv_w[j] for j in range(CONV_WIDTH))
    a, b = jnp.split(c, 2, axis=-1)
    return (jax.nn.silu(a) * b) @ w_down


def setup_inputs(seed: int = 0) -> dict:
    key = jax.random.key(seed)
    ks = jax.random.split(key, 14)
    f32 = jnp.float32

    def dense(k, shape, fan_in):
        return jax.random.normal(k, shape, f32) * (fan_in ** -0.5)

    def gain(k, shape):
        return 1.0 + 0.02 * jax.random.normal(k, shape, f32)

    NR, NA = N_RET_LAYERS, N_ATT_LAYERS
    return {
        "x": jax.random.normal(ks[0], (BATCH, SEQ, D_MODEL), f32),
        "norm_mix_g": gain(ks[1], (DEPTH, D_MODEL)),
        "ret_w_in": dense(ks[2], (NR, D_MODEL, RET_IN_DIM), D_MODEL),
        "ret_w_out": dense(ks[3], (NR, RET_HEADS * RET_V_DIM, D_MODEL), RET_HEADS * RET_V_DIM),
        "att_w_qkv": dense(ks[4], (NA, D_MODEL, ATT_IN_DIM), D_MODEL),
        "att_b_qkv": 0.02 * jax.random.normal(ks[5], (NA, ATT_IN_DIM), f32),
        "att_sinks": jax.random.normal(ks[6], (NA, ATT_HEADS), f32),
        "att_w_out": dense(ks[7], (NA, ATT_HEADS * ATT_HEAD_DIM, D_MODEL), ATT_HEADS * ATT_HEAD_DIM),
        "norm_ffn_g": gain(ks[8], (DEPTH, D_MODEL)),
        "ffn_w_up": dense(ks[9], (DEPTH, D_MODEL, 2 * D_FF), D_MODEL),
        "ffn_conv_w": dense(ks[10], (DEPTH, CONV_WIDTH, 2 * D_FF), CONV_WIDTH),
        "ffn_conv_b": 0.02 * jax.random.normal(ks[11], (DEPTH, 2 * D_FF), f32),
        "ffn_w_down": dense(ks[12], (DEPTH, D_FF, D_MODEL), D_FF),
        "final_norm_g": gain(ks[13], (D_MODEL,)),
    }


def reference(x, norm_mix_g, ret_w_in, ret_w_out, att_w_qkv, att_b_qkv, att_sinks, att_w_out,
              norm_ffn_g, ffn_w_up, ffn_conv_w, ffn_conv_b, ffn_w_down, final_norm_g):
    h = x
    for i in range(DEPTH):
        j = i // N_MIXERS
        hn = rmsnorm(h, norm_mix_g[i])
        if i % N_MIXERS == 0:
            h = h + retention(hn, ret_w_in[j], ret_w_out[j])
        else:
            h = h + sliding_window_attention(hn, att_w_qkv[j], att_b_qkv[j], att_sinks[j], att_w_out[j])
        hn = rmsnorm(h, norm_ffn_g[i])
        h = h + conv_ffn(hn, ffn_w_up[i], ffn_conv_w[i], ffn_conv_b[i], ffn_w_down[i])
    return rmsnorm(h, final_norm_g)
```

```python
import functools

import jax
import jax.numpy as jnp
from jax import lax
from jax.experimental import pallas as pl
from jax.experimental.pallas import tpu as pltpu

F32 = jnp.float32
BF16 = jnp.bfloat16

D_MODEL = 2048
N_MIXERS = 2

RET_HEADS = 8
RET_QK_DIM = D_MODEL // RET_HEADS
RET_V_DIM = 2 * D_MODEL // RET_HEADS
RET_CHUNK = 256

ATT_HEAD_DIM = 64
ATT_HEADS = D_MODEL // ATT_HEAD_DIM
ATT_KV_HEADS = ATT_HEADS // 8
ATT_GROUP = ATT_HEADS // ATT_KV_HEADS
WINDOW = 128
ATT_BLOCK = 128

D_FF = ((8 * D_MODEL // 3 + 127) // 128) * 128
CONV_WIDTH = 3
EPS = 1e-6

V7X_LANES = 128
V7X_SUBLANES = 8
V7X_VMEM_BYTES = 64 * 1024 * 1024
V7X_COMPILER_SCRATCH_BYTES = 8 * 1024 * 1024

FF_TILE = 512
D_FF_PAD = ((D_FF + FF_TILE - 1) // FF_TILE) * FF_TILE


def _nbytes(shape, dtype):
    n = 1
    for s in shape:
        n *= s
    return n * jnp.dtype(dtype).itemsize


def _vmem_limit(pipelined, resident=()):
    total = 2 * sum(_nbytes(s, d) for s, d in pipelined)
    total += sum(_nbytes(s, d) for s, d in resident)
    total += V7X_COMPILER_SCRATCH_BYTES
    assert total <= V7X_VMEM_BYTES, total
    return total


def _params(n_grid, vmem_limit):
    return pltpu.CompilerParams(
        dimension_semantics=("arbitrary",) * n_grid, vmem_limit_bytes=vmem_limit)


def _rmsnorm_kernel(x_ref, g_ref, o_ref):
    x = x_ref[...]
    ms = jnp.mean(x * x, axis=-1, keepdims=True)
    o_ref[...] = (x * lax.rsqrt(ms + EPS) * g_ref[...]).astype(o_ref.dtype)


def _rmsnorm(x, gain, *, tm=512):
    m, d = x.shape
    blocks = [((tm, d), F32), ((1, d), F32), ((tm, d), BF16)]
    return pl.pallas_call(
        _rmsnorm_kernel,
        grid=(m // tm,),
        in_specs=[pl.BlockSpec((tm, d), lambda i: (i, 0)),
                  pl.BlockSpec((1, d), lambda i: (0, 0))],
        out_specs=pl.BlockSpec((tm, d), lambda i: (i, 0)),
        out_shape=jax.ShapeDtypeStruct((m, d), BF16),
        compiler_params=_params(1, _vmem_limit(blocks)),
        name="rmsnorm",
    )(x, gain.reshape(1, d))


def _matmul_kernel(*refs, has_bias):
    if has_bias:
        x_ref, w_ref, b_ref, o_ref, wbf_ref = refs
    else:
        x_ref, w_ref, o_ref, wbf_ref = refs

    @pl.when(pl.program_id(1) == 0)
    def _():
        wbf_ref[...] = w_ref[...].astype(BF16)

    acc = jnp.dot(x_ref[...], wbf_ref[...], preferred_element_type=F32)
    if has_bias:
        acc = acc + b_ref[...]
    o_ref[...] = acc.astype(o_ref.dtype)


def _matmul(x, w, bias=None, *, tm, tn):
    m, k = x.shape
    n = w.shape[1]
    has_bias = bias is not None
    in_specs = [pl.BlockSpec((tm, k), lambda j, i: (i, 0)),
                pl.BlockSpec((k, tn), lambda j, i: (0, j))]
    args = [x, w]
    blocks = [((tm, k), BF16), ((k, tn), F32), ((tm, tn), BF16)]
    if has_bias:
        in_specs.append(pl.BlockSpec((1, tn), lambda j, i: (0, j)))
        args.append(bias.reshape(1, n))
        blocks.append(((1, tn), F32))
    return pl.pallas_call(
        functools.partial(_matmul_kernel, has_bias=has_bias),
        grid=(n // tn, m // tm),
        in_specs=in_specs,
        out_specs=pl.BlockSpec((tm, tn), lambda j, i: (i, j)),
        out_shape=jax.ShapeDtypeStruct((m, n), BF16),
        scratch_shapes=[pltpu.VMEM((k, tn), BF16)],
        compiler_params=_params(2, _vmem_limit(blocks, [((k, tn), BF16)])),
        name="matmul_bias" if has_bias else "matmul",
    )(*args)


def _retention_kernel(lg_ref, q_ref, k_ref, v_ref, g_ref, y_ref,
                      state_ref, dmat_ref, qd_ref, kd_ref, cd_ref):
    h = pl.program_id(0)
    c = pl.program_id(1)
    cs = q_ref.shape[0]
    k_scale = RET_QK_DIM ** -0.5

    @pl.when(c == 0)
    def _():
        lg = lg_ref[h]
        n = lax.broadcasted_iota(jnp.int32, (cs, cs), 0)
        m = lax.broadcasted_iota(jnp.int32, (cs, cs), 1)
        diff = (n - m).astype(F32)
        dmat_ref[...] = jnp.where(diff >= 0, jnp.exp(lg * jnp.maximum(diff, 0.0)), 0.0) * k_scale
        r = lax.broadcasted_iota(jnp.int32, (cs, 1), 0).astype(F32)
        qd_ref[...] = jnp.exp(lg * (r + 1.0))
        kd_ref[...] = jnp.exp(lg * (cs - 1.0 - r)) * k_scale
        cd_ref[...] = jnp.exp(jnp.full(cd_ref.shape, lg * cs, F32))
        state_ref[...] = jnp.zeros_like(state_ref)

    q = q_ref[...]
    k = k_ref[...]
    v = v_ref[...]
    s = lax.dot_general(q, k, (((1,), (1,)), ((), ())), preferred_element_type=F32)
    p = (s * dmat_ref[...]).astype(BF16)
    qs = (q.astype(F32) * qd_ref[...]).astype(BF16)
    state = state_ref[...]
    o = (jnp.dot(p, v, preferred_element_type=F32)
         + jnp.dot(qs, state.astype(BF16), preferred_element_type=F32))
    ks = (k.astype(F32) * kd_ref[...]).astype(BF16)
    upd = lax.dot_general(ks, v, (((0,), (0,)), ((), ())), preferred_element_type=F32)
    state_ref[...] = state * cd_ref[...] + upd

    ms = jnp.mean(o * o, axis=-1, keepdims=True)
    on = o * lax.rsqrt(ms + EPS)
    g = g_ref[...].astype(F32)
    y_ref[...] = (g / (1.0 + jnp.exp(-g)) * on).astype(y_ref.dtype)


def _retention(proj):
    s = proj.shape[0]
    hh, dk, dv, cs = RET_HEADS, RET_QK_DIM, RET_V_DIM, RET_CHUNK
    log_gamma = jnp.log1p(-(2.0 ** (-5.0 - jnp.arange(hh, dtype=F32))))
    k_blk0 = hh * dk // dk
    v_blk0 = 2 * hh * dk // dv
    g_blk0 = v_blk0 + hh
    blocks = [((cs, dk), BF16)] * 2 + [((cs, dv), BF16)] * 3
    scratch = [((dk, dv), F32), ((cs, cs), F32), ((cs, V7X_LANES), F32),
               ((cs, V7X_LANES), F32), ((V7X_SUBLANES, dv), F32)]
    return pl.pallas_call(
        _retention_kernel,
        grid=(hh, s // cs),
        in_specs=[pl.BlockSpec(memory_space=pltpu.SMEM),
                  pl.BlockSpec((cs, dk), lambda h, c: (c, h)),
                  pl.BlockSpec((cs, dk), lambda h, c: (c, k_blk0 + h)),
                  pl.BlockSpec((cs, dv), lambda h, c: (c, v_blk0 + h)),
                  pl.BlockSpec((cs, dv), lambda h, c: (c, g_blk0 + h))],
        out_specs=pl.BlockSpec((cs, dv), lambda h, c: (c, h)),
        out_shape=jax.ShapeDtypeStruct((s, hh * dv), BF16),
        scratch_shapes=[pltpu.VMEM((dk, dv), F32),
                        pltpu.VMEM((cs, cs), F32),
                        pltpu.VMEM((cs, 1), F32),
                        pltpu.VMEM((cs, 1), F32),
                        pltpu.VMEM((1, dv), F32)],
        compiler_params=_params(2, _vmem_limit(blocks, scratch)),
        name="retention",
    )(log_gamma, proj, proj, proj, proj)


def _proj_res_norm_kernel(a_ref, w_ref, res_ref, gain_ref, *out_refs, emit_residual):
    h = res_ref[...] + jnp.dot(a_ref[...], w_ref[...], preferred_element_type=F32)
    ms = jnp.mean(h * h, axis=-1, keepdims=True)
    hn = h * lax.rsqrt(ms + EPS) * gain_ref[...]
    if emit_residual:
        h_ref, hn_ref = out_refs
        h_ref[...] = h
    else:
        (hn_ref,) = out_refs
    hn_ref[...] = hn.astype(hn_ref.dtype)


def _proj_res_norm(a, w, res, gain, *, emit_residual, tm=256):
    m, k = a.shape
    d = w.shape[1]
    row = lambda i: (i, 0)
    if emit_residual:
        out_shape = (jax.ShapeDtypeStruct((m, d), F32), jax.ShapeDtypeStruct((m, d), BF16))
        out_specs = (pl.BlockSpec((tm, d), row), pl.BlockSpec((tm, d), row))
        out_blocks = [((tm, d), F32), ((tm, d), BF16)]
    else:
        out_shape = jax.ShapeDtypeStruct((m, d), F32)
        out_specs = pl.BlockSpec((tm, d), row)
        out_blocks = [((tm, d), F32)]
    blocks = [((tm, k), BF16), ((tm, d), F32), ((1, d), F32)] + out_blocks
    return pl.pallas_call(
        functools.partial(_proj_res_norm_kernel, emit_residual=emit_residual),
        grid=(m // tm,),
        in_specs=[pl.BlockSpec((tm, k), row),
                  pl.BlockSpec((k, d), lambda i: (0, 0), pipeline_mode=pl.Buffered(1)),
                  pl.BlockSpec((tm, d), row),
                  pl.BlockSpec((1, d), lambda i: (0, 0))],
        out_specs=out_specs,
        out_shape=out_shape,
        compiler_params=_params(1, _vmem_limit(blocks, [((k, d), BF16)])),
        name="proj_res_norm" if emit_residual else "proj_res_final_norm",
    )(a, w, res, gain.reshape(1, d))


def _ffn_up_kernel(x_ref, wa_ref, wb_ref, cwa_ref, cwb_ref, cba_ref, cbb_ref, g_ref,
                   carry_a_ref, carry_b_ref):
    tm = x_ref.shape[0]

    @pl.when(pl.program_id(1) == 0)
    def _():
        carry_a_ref[...] = jnp.zeros_like(carry_a_ref)
        carry_b_ref[...] = jnp.zeros_like(carry_b_ref)

    x = x_ref[...]

    def conv(w_ref, cw_ref, cb_ref, carry_ref):
        u = jnp.dot(x, w_ref[...], preferred_element_type=F32)
        row = lax.broadcasted_iota(jnp.int32, u.shape, 0)
        prev = carry_ref[...]
        last = prev[V7X_SUBLANES - 1:V7X_SUBLANES]
        u1 = jnp.where(row == 0, last, pltpu.roll(u, 1, axis=0))
        u2 = jnp.where(row == 0, prev[V7X_SUBLANES - 2:V7X_SUBLANES - 1],
                       jnp.where(row == 1, last, pltpu.roll(u, 2, axis=0)))
        carry_ref[...] = u[tm - V7X_SUBLANES:]
        cw = cw_ref[...]
        return cb_ref[...] + (cw[0:1] * u2 + cw[1:2] * u1 + cw[2:3] * u)

    a = conv(wa_ref, cwa_ref, cba_ref, carry_a_ref)
    b = conv(wb_ref, cwb_ref, cbb_ref, carry_b_ref)
    g_ref[...] = (a / (1.0 + jnp.exp(-a)) * b).astype(g_ref.dtype)


def _ffn_up(x, w_up, conv_w, conv_b, *, tm=512, tf=FF_TILE):
    m, d = x.shape
    nf = D_FF_PAD // tf
    blocks = ([((tm, d), BF16)] + [((d, tf), BF16)] * 2 + [((CONV_WIDTH, tf), F32)] * 2
              + [((1, tf), F32)] * 2 + [((tm, tf), BF16)])
    scratch = [((V7X_SUBLANES, tf), F32)] * 2
    return pl.pallas_call(
        _ffn_up_kernel,
        grid=(nf, m // tm),
        in_specs=[pl.BlockSpec((tm, d), lambda j, i: (i, 0)),
                  pl.BlockSpec((d, tf), lambda j, i: (0, j)),
                  pl.BlockSpec((d, tf), lambda j, i: (0, nf + j)),
                  pl.BlockSpec((CONV_WIDTH, tf), lambda j, i: (0, j)),
                  pl.BlockSpec((CONV_WIDTH, tf), lambda j, i: (0, nf + j)),
                  pl.BlockSpec((1, tf), lambda j, i: (0, j)),
                  pl.BlockSpec((1, tf), lambda j, i: (0, nf + j))],
        out_specs=pl.BlockSpec((tm, tf), lambda j, i: (i, j)),
        out_shape=jax.ShapeDtypeStruct((m, D_FF_PAD), BF16),
        scratch_shapes=[pltpu.VMEM((V7X_SUBLANES, tf), F32)] * 2,
        compiler_params=_params(2, _vmem_limit(blocks, scratch)),
        name="ffn_up",
    )(x, w_up, w_up, conv_w, conv_w, conv_b, conv_b)


def _pad_ff_halves(t, dtype):
    lead = t.shape[:-1]
    t = t.reshape(lead + (2, D_FF))
    t = jnp.pad(t, [(0, 0)] * len(lead) + [(0, 0), (0, D_FF_PAD - D_FF)])
    return t.reshape(lead + (2 * D_FF_PAD,)).astype(dtype)


def _swa_kernel(slopes_ref, sinks_ref, q_ref, kp_ref, kc_ref, vp_ref, vc_ref, o_ref):
    n = pl.program_id(0)
    blk, dh, grp = ATT_BLOCK, ATT_HEAD_DIM, ATT_GROUP
    lanes = 2 * dh
    qi = lax.broadcasted_iota(jnp.int32, (blk, 2 * blk), 0)
    kj = lax.broadcasted_iota(jnp.int32, (blk, 2 * blk), 1)
    dist = blk + qi - kj
    valid = (dist >= 0) & (dist < WINDOW) & ((n > 0) | (kj >= blk))
    dist_f = dist.astype(F32)
    lane_kv = lax.broadcasted_iota(jnp.int32, (2 * blk, lanes), 1)
    lane_o = lax.broadcasted_iota(jnp.int32, (blk, lanes), 1)

    for pair in range(ATT_KV_HEADS // 2):
        cols = slice(pair * lanes, (pair + 1) * lanes)
        k2 = jnp.concatenate([kp_ref[:, cols], kc_ref[:, cols]], axis=0).astype(F32) * (dh ** -0.5)
        v2 = jnp.concatenate([vp_ref[:, cols], vc_ref[:, cols]], axis=0).astype(F32)
        k2s = pltpu.roll(k2, dh, axis=1)
        v2s = pltpu.roll(v2, dh, axis=1)
        for sub in range(2):
            kvh = 2 * pair + sub
            k_lo, k_hi = (k2, k2s) if sub == 0 else (k2s, k2)
            v_lo, v_hi = (v2, v2s) if sub == 0 else (v2s, v2)
            kz = (jnp.where(lane_kv < dh, k_lo, 0.0).astype(BF16), jnp.where(lane_kv >= dh, k_hi, 0.0).astype(BF16))
            vz = (jnp.where(lane_kv < dh, v_lo, 0.0).astype(BF16), jnp.where(lane_kv >= dh, v_hi, 0.0).astype(BF16))
            for hp in range(grp // 2):
                qcols = slice((kvh * grp + 2 * hp) * dh, (kvh * grp + 2 * hp + 2) * dh)
                qp = q_ref[:, qcols]
                acc = None
                inv = []
                for par in range(2):
                    head = kvh * grp + 2 * hp + par
                    s = lax.dot_general(qp, kz[par], (((1,), (1,)), ((), ())), preferred_element_type=F32)
                    s = jnp.where(valid, s - slopes_ref[head] * dist_f, -jnp.inf)
                    sink = sinks_ref[head]
                    m = jnp.maximum(jnp.max(s, axis=-1, keepdims=True), sink)
                    e = jnp.exp(s - m)
                    denom = jnp.sum(e, axis=-1, keepdims=True) + jnp.exp(sink - m)
                    inv.append(1.0 / denom)
                    pv = jnp.dot(e.astype(BF16), vz[par], preferred_element_type=F32)
                    acc = pv if acc is None else acc + pv
                o_ref[:, qcols] = (acc * jnp.where(lane_o < dh, inv[0], inv[1])).astype(o_ref.dtype)


def _swa(qkv, sinks):
    s = qkv.shape[0]
    blk = ATT_BLOCK
    dq = ATT_HEADS * ATT_HEAD_DIM
    dkv = ATT_KV_HEADS * ATT_HEAD_DIM
    k_blk = dq // dkv
    v_blk = k_blk + 1
    slopes = 2.0 ** (-8.0 * jnp.arange(1, ATT_HEADS + 1, dtype=F32) / ATT_HEADS)
    prev = lambda n: jnp.maximum(n - 1, 0)
    blocks = [((blk, dq), BF16)] * 2 + [((blk, dkv), BF16)] * 4
    return pl.pallas_call(
        _swa_kernel,
        grid=(s // blk,),
        in_specs=[pl.BlockSpec(memory_space=pltpu.SMEM),
                  pl.BlockSpec(memory_space=pltpu.SMEM),
                  pl.BlockSpec((blk, dq), lambda n: (n, 0)),
                  pl.BlockSpec((blk, dkv), lambda n: (prev(n), k_blk)),
                  pl.BlockSpec((blk, dkv), lambda n: (n, k_blk)),
                  pl.BlockSpec((blk, dkv), lambda n: (prev(n), v_blk)),
                  pl.BlockSpec((blk, dkv), lambda n: (n, v_blk))],
        out_specs=pl.BlockSpec((blk, dq), lambda n: (n, 0)),
        out_shape=jax.ShapeDtypeStruct((s, dq), BF16),
        compiler_params=_params(1, _vmem_limit(blocks)),
        name="swa",
    )(slopes, sinks.astype(F32), qkv, qkv, qkv, qkv, qkv)


def kernel(x, norm_mix_g, ret_w_in, ret_w_out, att_w_qkv, att_b_qkv, att_sinks, att_w_out,
           norm_ffn_g, ffn_w_up, ffn_conv_w, ffn_conv_b, ffn_w_down, final_norm_g):
    batch, seq, d = x.shape
    assert batch == 1 and d == D_MODEL
    depth = norm_mix_g.shape[0]

    h = x.reshape(seq, d)
    hn = _rmsnorm(h, norm_mix_g[0])
    for i in range(depth):
        j = i // N_MIXERS
        if i % N_MIXERS == 0:
            proj = _matmul(hn, ret_w_in[j], tm=1024, tn=1024)
            mixed = _retention(proj)
            w_out = ret_w_out[j]
        else:
            qkv = _matmul(hn, att_w_qkv[j], att_b_qkv[j], tm=1024, tn=512)
            mixed = _swa(qkv, att_sinks[j])
            w_out = att_w_out[j]
        h, hn = _proj_res_norm(mixed, w_out.astype(BF16), h, norm_ffn_g[i], emit_residual=True)

        w_up = _pad_ff_halves(ffn_w_up[i], BF16)
        conv_w = _pad_ff_halves(ffn_conv_w[i], F32)
        conv_b = _pad_ff_halves(ffn_conv_b[i].reshape(1, -1), F32)
        w_down = jnp.pad(ffn_w_down[i], ((0, D_FF_PAD - D_FF), (0, 0))).astype(BF16)
        gated = _ffn_up(hn, w_up, conv_w, conv_b)
        if i + 1 < depth:
            h, hn = _proj_res_norm(gated, w_down, h, norm_mix_g[i + 1], emit_residual=True)
        else:
            out = _proj_res_norm(gated, w_down, h, final_norm_g, emit_residual=False)
    return out.reshape(batch, seq, d)
```

```python
import functools

import jax
import jax.numpy as jnp
from jax import lax
from jax.experimental import pallas as pl
from jax.experimental.pallas import tpu as pltpu

F32 = jnp.float32
BF16 = jnp.bfloat16

D_MODEL = 2048
N_MIXERS = 2

RET_HEADS = 8
RET_QK_DIM = D_MODEL // RET_HEADS
RET_V_DIM = 2 * D_MODEL // RET_HEADS
RET_CHUNK = 256

ATT_HEAD_DIM = 64
ATT_HEADS = D_MODEL // ATT_HEAD_DIM
ATT_KV_HEADS = ATT_HEADS // 8
ATT_GROUP = ATT_HEADS // ATT_KV_HEADS
WINDOW = 128
ATT_BLOCK = 128

D_FF = ((8 * D_MODEL // 3 + 127) // 128) * 128
CONV_WIDTH = 3
EPS = 1e-6

V7X_LANES = 128
V7X_SUBLANES = 8
V7X_VMEM_BYTES = 64 * 1024 * 1024
V7X_COMPILER_SCRATCH_BYTES = 8 * 1024 * 1024

FF_TILE = 512
D_FF_PAD = ((D_FF + FF_TILE - 1) // FF_TILE) * FF_TILE

D_MODEL_PITCH = D_MODEL + V7X_LANES


def _nbytes(shape, dtype):
    n = 1
    for s in shape:
        n *= s
    return n * jnp.dtype(dtype).itemsize


def _vmem_limit(pipelined, resident=()):
    total = 2 * sum(_nbytes(s, d) for s, d in pipelined)
    total += sum(_nbytes(s, d) for s, d in resident)
    total += V7X_COMPILER_SCRATCH_BYTES
    assert total <= V7X_VMEM_BYTES, total
    return total


def _params(n_grid, vmem_limit):
    return pltpu.CompilerParams(
        dimension_semantics=("arbitrary",) * n_grid, vmem_limit_bytes=vmem_limit)


def _store_padded(ref, value):
    d = value.shape[1]
    ref[:, :d] = value.astype(ref.dtype)
    ref[:, d:] = jnp.zeros((ref.shape[0], ref.shape[1] - d), ref.dtype)


def _rmsnorm_kernel(x_ref, g_ref, o_ref):
    x = x_ref[...]
    ms = jnp.mean(x * x, axis=-1, keepdims=True)
    _store_padded(o_ref, x * lax.rsqrt(ms + EPS) * g_ref[...])


def _rmsnorm(x, gain, *, tm=512):
    m, d = x.shape
    blocks = [((tm, d), F32), ((1, d), F32), ((tm, D_MODEL_PITCH), BF16)]
    return pl.pallas_call(
        _rmsnorm_kernel,
        grid=(m // tm,),
        in_specs=[pl.BlockSpec((tm, d), lambda i: (i, 0)),
                  pl.BlockSpec((1, d), lambda i: (0, 0))],
        out_specs=pl.BlockSpec((tm, D_MODEL_PITCH), lambda i: (i, 0)),
        out_shape=jax.ShapeDtypeStruct((m, D_MODEL_PITCH), BF16),
        compiler_params=_params(1, _vmem_limit(blocks)),
        name="rmsnorm",
    )(x, gain.reshape(1, d))


def _matmul_kernel(*refs, has_bias):
    if has_bias:
        x_ref, w_ref, b_ref, o_ref, wbf_ref = refs
    else:
        x_ref, w_ref, o_ref, wbf_ref = refs

    @pl.when(pl.program_id(1) == 0)
    def _():
        wbf_ref[...] = w_ref[...].astype(BF16)

    k = wbf_ref.shape[0]
    acc = jnp.dot(x_ref[:, :k], wbf_ref[...], preferred_element_type=F32)
    if has_bias:
        acc = acc + b_ref[...]
    o_ref[...] = acc.astype(o_ref.dtype)


def _matmul(x, w, layer, bias=None, *, tm, tn):
    m, pitch = x.shape
    k, n = w.shape[1:]
    has_bias = bias is not None
    in_specs = [pl.BlockSpec((tm, pitch), lambda j, i: (i, 0)),
                pl.BlockSpec((None, k, tn), lambda j, i: (layer, 0, j))]
    args = [x, w]
    blocks = [((tm, pitch), BF16), ((k, tn), F32), ((tm, tn), BF16)]
    if has_bias:
        in_specs.append(pl.BlockSpec((None, 1, tn), lambda j, i: (layer, 0, j)))
        args.append(bias.reshape(bias.shape[0], 1, n))
        blocks.append(((1, tn), F32))
    return pl.pallas_call(
        functools.partial(_matmul_kernel, has_bias=has_bias),
        grid=(n // tn, m // tm),
        in_specs=in_specs,
        out_specs=pl.BlockSpec((tm, tn), lambda j, i: (i, j)),
        out_shape=jax.ShapeDtypeStruct((m, n), BF16),
        scratch_shapes=[pltpu.VMEM((k, tn), BF16)],
        compiler_params=_params(2, _vmem_limit(blocks, [((k, tn), BF16)])),
        name="matmul_bias" if has_bias else "matmul",
    )(*args)


def _ret_decay_tables():
    hh, dk, cs = RET_HEADS, RET_QK_DIM, RET_CHUNK
    log_gamma = jnp.log1p(-(2.0 ** (-5.0 - jnp.arange(hh, dtype=F32))))
    n = jnp.arange(cs, dtype=F32)[:, None]
    q_scale = jnp.exp(log_gamma[None, :] * (n + 1.0 - cs))
    k_scale = jnp.exp(log_gamma[None, :] * (cs - 1.0 - n)) * (dk ** -0.5)
    table = jnp.concatenate([jnp.repeat(q_scale, dk, axis=1), jnp.repeat(k_scale, dk, axis=1)], axis=1)
    chunk_decay = jnp.exp(log_gamma * cs)
    return table, chunk_decay


RET_PROJ_SUB_ROWS = 256


def _ret_in_proj_kernel(x_ref, w_ref, tab_ref, o_ref, wbf_ref, *, n_scaled, n_plain):
    j = pl.program_id(0)
    tm = x_ref.shape[0]
    cs = tab_ref.shape[0]

    @pl.when(pl.program_id(1) == 0)
    def _():
        wbf_ref[...] = w_ref[...].astype(BF16)

    def run(epilogue):
        w = wbf_ref[...]
        k = w.shape[0]
        for r0 in range(0, tm, RET_PROJ_SUB_ROWS):
            acc = jnp.dot(x_ref[r0:r0 + RET_PROJ_SUB_ROWS, :k], w, preferred_element_type=F32)
            o_ref[r0:r0 + RET_PROJ_SUB_ROWS, :] = epilogue(acc, r0).astype(o_ref.dtype)

    def scaled(acc, r0):
        del r0
        return acc * tab_ref[...]

    @pl.when(j < n_scaled)
    def _():
        run(scaled)

    @pl.when((j >= n_scaled) & (j < n_scaled + n_plain))
    def _():
        run(lambda acc, r0: acc)

    @pl.when(j >= n_scaled + n_plain)
    def _():
        run(lambda acc, r0: acc / (1.0 + jnp.exp(-acc)))


def _ret_in_proj(x, w, layer, table, *, tm=1024, tn=1024):
    m, pitch = x.shape
    k, n = w.shape[1:]
    cs, scaled_cols = table.shape
    assert scaled_cols % tn == 0 and RET_PROJ_SUB_ROWS == cs
    n_scaled = scaled_cols // tn
    n_plain = RET_HEADS * RET_V_DIM // tn
    blocks = [((tm, pitch), BF16), ((k, tn), F32), ((cs, tn), F32), ((tm, tn), BF16)]
    return pl.pallas_call(
        functools.partial(_ret_in_proj_kernel, n_scaled=n_scaled, n_plain=n_plain),
        grid=(n // tn, m // tm),
        in_specs=[pl.BlockSpec((tm, pitch), lambda j, i: (i, 0)),
                  pl.BlockSpec((None, k, tn), lambda j, i: (layer, 0, j)),
                  pl.BlockSpec((cs, tn), lambda j, i: (0, jnp.minimum(j, n_scaled - 1)))],
        out_specs=pl.BlockSpec((tm, tn), lambda j, i: (i, j)),
        out_shape=jax.ShapeDtypeStruct((m, n), BF16),
        scratch_shapes=[pltpu.VMEM((k, tn), BF16)],
        compiler_params=_params(2, _vmem_limit(blocks, [((k, tn), BF16)])),
        name="ret_in_proj",
    )(x, w, table)


def _retention_kernel(cd_ref, q_ref, k_ref, v_ref, g_ref, y_ref, state_ref, sbf_ref):
    hh, dk, dv = RET_HEADS, RET_QK_DIM, RET_V_DIM
    cs = q_ref.shape[0]

    @pl.when(pl.program_id(0) == 0)
    def _():
        state_ref[...] = jnp.zeros_like(state_ref)
        sbf_ref[...] = jnp.zeros_like(sbf_ref)

    causal = (lax.broadcasted_iota(jnp.int32, (cs, cs), 0)
              >= lax.broadcasted_iota(jnp.int32, (cs, cs), 1))
    for h in range(hh):
        q = q_ref[:, h * dk:(h + 1) * dk]
        k = k_ref[:, h * dk:(h + 1) * dk]
        v = v_ref[:, h * dv:(h + 1) * dv]
        s = lax.dot_general(q, k, (((1,), (1,)), ((), ())), preferred_element_type=F32)
        p = jnp.where(causal, s, 0.0).astype(BF16)
        o = (jnp.dot(p, v, preferred_element_type=F32)
             + jnp.dot(q, sbf_ref[h], preferred_element_type=F32))
        upd = lax.dot_general(k, v, (((0,), (0,)), ((), ())), preferred_element_type=F32)
        new_state = (state_ref[h] + upd) * cd_ref[h]
        state_ref[h] = new_state
        sbf_ref[h] = new_state.astype(BF16)

        ms = jnp.mean(o * o, axis=-1, keepdims=True)
        gate = g_ref[:, h * dv:(h + 1) * dv].astype(F32)
        y_ref[:, h * dv:(h + 1) * dv] = (gate * (o * lax.rsqrt(ms + EPS))).astype(y_ref.dtype)


def _retention(proj, chunk_decay):
    s = proj.shape[0]
    hh, dk, dv, cs = RET_HEADS, RET_QK_DIM, RET_V_DIM, RET_CHUNK
    qk_w, v_w = hh * dk, hh * dv
    assert 2 * qk_w == v_w
    blocks = [((cs, qk_w), BF16)] * 2 + [((cs, v_w), BF16)] * 3
    scratch = [((hh, dk, dv), F32), ((hh, dk, dv), BF16)]
    return pl.pallas_call(
        _retention_kernel,
        grid=(s // cs,),
        in_specs=[pl.BlockSpec(memory_space=pltpu.SMEM),
                  pl.BlockSpec((cs, qk_w), lambda c: (c, 0)),
                  pl.BlockSpec((cs, qk_w), lambda c: (c, 1)),
                  pl.BlockSpec((cs, v_w), lambda c: (c, 1)),
                  pl.BlockSpec((cs, v_w), lambda c: (c, 2))],
        out_specs=pl.BlockSpec((cs, v_w), lambda c: (c, 0)),
        out_shape=jax.ShapeDtypeStruct((s, v_w), BF16),
        scratch_shapes=[pltpu.VMEM((hh, dk, dv), F32), pltpu.VMEM((hh, dk, dv), BF16)],
        compiler_params=_params(1, _vmem_limit(blocks, scratch)),
        name="retention",
    )(chunk_decay, proj, proj, proj, proj)


def _proj_res_norm_kernel(a_ref, w_ref, res_ref, gain_ref, *rest, emit_residual, n_wchunks):
    out_refs, wbf_ref = rest[:-1], rest[-1]
    step = pl.program_id(0)
    tk = w_ref.shape[0]

    @pl.when(step < n_wchunks)
    def _():
        row0 = pl.multiple_of(step * tk, tk)
        wbf_ref[pl.ds(row0, tk), :] = w_ref[...].astype(BF16)

    @pl.when(step >= n_wchunks)
    def _():
        h = res_ref[...] + jnp.dot(a_ref[...], wbf_ref[...], preferred_element_type=F32)
        ms = jnp.mean(h * h, axis=-1, keepdims=True)
        hn = h * lax.rsqrt(ms + EPS) * gain_ref[...]
        if emit_residual:
            h_ref, hn_ref = out_refs
            h_ref[...] = h
            _store_padded(hn_ref, hn)
        else:
            (hn_ref,) = out_refs
            hn_ref[...] = hn.astype(hn_ref.dtype)


def _proj_res_norm(a, w, layer, res, gain, *, emit_residual, tk, tm=256):
    m = a.shape[0]
    k, d = w.shape[1:]
    assert k % tk == 0 and tk % (2 * V7X_SUBLANES) == 0
    nw = k // tk
    row = lambda s: (jnp.maximum(s - nw, 0), 0)
    if emit_residual:
        out_shape = (jax.ShapeDtypeStruct((m, d), F32), jax.ShapeDtypeStruct((m, D_MODEL_PITCH), BF16))
        out_specs = (pl.BlockSpec((tm, d), row), pl.BlockSpec((tm, D_MODEL_PITCH), row))
        out_blocks = [((tm, d), F32), ((tm, D_MODEL_PITCH), BF16)]
    else:
        out_shape = jax.ShapeDtypeStruct((m, d), F32)
        out_specs = pl.BlockSpec((tm, d), row)
        out_blocks = [((tm, d), F32)]
    blocks = [((tm, k), BF16), ((tk, d), F32), ((tm, d), F32), ((1, d), F32)] + out_blocks
    return pl.pallas_call(
        functools.partial(_proj_res_norm_kernel, emit_residual=emit_residual, n_wchunks=nw),
        grid=(nw + m // tm,),
        in_specs=[pl.BlockSpec((tm, k), row),
                  pl.BlockSpec((None, tk, d), lambda s: (layer, jnp.minimum(s, nw - 1), 0)),
                  pl.BlockSpec((tm, d), row),
                  pl.BlockSpec((1, d), lambda s: (0, 0))],
        out_specs=out_specs,
        out_shape=out_shape,
        scratch_shapes=[pltpu.VMEM((k, d), BF16)],
        compiler_params=_params(1, _vmem_limit(blocks, [((k, d), BF16)])),
        name="proj_res_norm" if emit_residual else "proj_res_final_norm",
    )(a, w, res, gain.reshape(1, d))


FF_FULL_TILES = D_FF // FF_TILE
FF_REM = D_FF % FF_TILE
FF_NUM_TILES = FF_FULL_TILES + 1
FF_EPI_ROWS = 32


FF_SUB_ROWS = 256


def _ffn_up_kernel(x_ref, wa_ref, wb0_ref, wb1_ref, cw_ref, cb_ref, g_ref, wbf_ref, u0_ref, u1_ref, u2_ref):
    j = pl.program_id(0)
    i = pl.program_id(1)
    tm = x_ref.shape[0]
    tf = wa_ref.shape[1]
    head = wb0_ref.shape[1]
    hist = V7X_SUBLANES

    @pl.when((i == 0) & (j == 0))
    def _():
        u1_ref[tm:, :] = jnp.zeros((hist, 2 * tf), F32)
        u2_ref[tm:, :] = jnp.zeros((hist, 2 * tf), F32)

    @pl.when(i == 0)
    def _():
        wbf_ref[:, :tf] = wa_ref[...].astype(BF16)
        wbf_ref[:, tf:tf + head] = wb0_ref[...].astype(BF16)
        u1_ref[0:hist, :] = jnp.zeros((hist, 2 * tf), F32)
        u2_ref[0:hist, :] = jnp.zeros((hist, 2 * tf), F32)

    @pl.when((i == 0) & (j < FF_FULL_TILES))
    def _():
        wbf_ref[:, tf + head:] = wb1_ref[:, :FF_REM].astype(BF16)

    @pl.when((i == 0) & (j == FF_FULL_TILES))
    def _():
        valid = FF_REM - head
        wbf_ref[:, tf + head:tf + head + valid] = wb1_ref[:, :valid].astype(BF16)
        wbf_ref[:, tf + head + valid:] = jnp.zeros((wbf_ref.shape[0], tf - head - valid), BF16)

    def conv(r0, c0):
        rows = slice(r0, r0 + FF_EPI_ROWS)
        cols = slice(c0, c0 + V7X_LANES)
        cw = cw_ref[:, cols]
        return cb_ref[:, cols] + (cw[0:1] * u2_ref[rows, cols] + cw[1:2] * u1_ref[rows, cols]
                                  + cw[2:3] * u0_ref[rows, cols])

    w = wbf_ref[...]
    for s0 in range(0, tm, FF_SUB_ROWS):
        u = jnp.dot(x_ref[s0:s0 + FF_SUB_ROWS, :w.shape[0]], w, preferred_element_type=F32)
        u0_ref[s0:s0 + FF_SUB_ROWS, :] = u
        u1_ref[s0 + 1:s0 + 1 + FF_SUB_ROWS, :] = u
        u2_ref[s0 + 2:s0 + 2 + FF_SUB_ROWS, :] = u
        for r0 in range(s0, s0 + FF_SUB_ROWS, FF_EPI_ROWS):
            for c0 in range(0, tf, V7X_LANES):
                a = conv(r0, c0)
                b = conv(r0, tf + c0)
                g_ref[r0:r0 + FF_EPI_ROWS, c0:c0 + V7X_LANES] = (
                    a / (1.0 + jnp.exp(-a)) * b).astype(g_ref.dtype)
    u1_ref[0:hist, :] = u1_ref[tm:, :]
    u2_ref[0:hist, :] = u2_ref[tm:, :]


def _ffn_up(x, w_up, layer, conv_w, conv_b, *, tm=1024, tf=FF_TILE):
    m, pitch = x.shape
    d = w_up.shape[1]
    nf = FF_NUM_TILES
    head = tf - FF_REM
    assert tf % head == 0 and FF_REM % head == 0
    def per_tile(t):
        taps = t.shape[0]
        t = jnp.pad(t.reshape(taps, 2, D_FF), ((0, 0), (0, 0), (0, D_FF_PAD - D_FF)))
        return t.reshape(taps, 2, nf, tf).transpose(2, 0, 1, 3).reshape(nf, taps, 2 * tf)
    cw = per_tile(conv_w)
    cb = per_tile(conv_b.reshape(1, -1))
    blocks = ([((tm, pitch), BF16)] + [((d, tf), F32)] * 2 + [((d, head), F32)]
              + [((CONV_WIDTH, 2 * tf), F32)] + [((1, 2 * tf), F32)] + [((tm, tf), BF16)])
    scratch = [((d, 2 * tf), BF16), ((tm, 2 * tf), F32)] + [((tm + V7X_SUBLANES, 2 * tf), F32)] * 2
    return pl.pallas_call(
        _ffn_up_kernel,
        grid=(nf, m // tm),
        in_specs=[pl.BlockSpec((tm, pitch), lambda j, i: (i, 0)),
                  pl.BlockSpec((None, d, tf), lambda j, i: (layer, 0, j)),
                  pl.BlockSpec((None, d, head),
                               lambda j, i: (layer, 0, (FF_FULL_TILES + j) * (tf // head) + FF_REM // head)),
                  pl.BlockSpec((None, d, tf), lambda j, i: (layer, 0, FF_FULL_TILES + 1 + j)),
                  pl.BlockSpec((None, CONV_WIDTH, 2 * tf), lambda j, i: (j, 0, 0)),
                  pl.BlockSpec((None, 1, 2 * tf), lambda j, i: (j, 0, 0))],
        out_specs=pl.BlockSpec((tm, tf), lambda j, i: (i, j)),
        out_shape=jax.ShapeDtypeStruct((m, D_FF_PAD), BF16),
        scratch_shapes=[pltpu.VMEM(*shape_dtype) for shape_dtype in scratch],
        compiler_params=_params(2, _vmem_limit(blocks, scratch)),
        name="ffn_up",
    )(x, w_up, w_up, w_up, cw, cb)


def _swa_kernel(slopes_ref, sinks_ref, q_ref, kp_ref, kc_ref, vp_ref, vc_ref, o_ref, bias_ref):
    n = pl.program_id(0)
    blk, dh, grp = ATT_BLOCK, ATT_HEAD_DIM, ATT_GROUP
    lanes = 2 * dh
    pairs = grp // 2

    @pl.when(n <= 1)
    def _():
        qi = lax.broadcasted_iota(jnp.int32, (blk, 2 * blk), 0)
        kj = lax.broadcasted_iota(jnp.int32, (blk, 2 * blk), 1)
        dist = blk + qi - kj
        valid = (dist >= 0) & (dist < WINDOW) & ((n > 0) | (kj >= blk))
        dist_f = dist.astype(F32)
        for head in range(ATT_HEADS):
            bias_ref[head] = jnp.where(valid, -(slopes_ref[head] * dist_f), -jnp.inf)

    lane_kv = lax.broadcasted_iota(jnp.int32, (2 * blk, lanes), 1)
    lane_o = lax.broadcasted_iota(jnp.int32, (blk, lanes), 1)

    for pair in range(ATT_KV_HEADS // 2):
        cols = slice(pair * lanes, (pair + 1) * lanes)
        k2 = jnp.concatenate([kp_ref[:, cols], kc_ref[:, cols]], axis=0).astype(F32) * (dh ** -0.5)
        v2 = jnp.concatenate([vp_ref[:, cols], vc_ref[:, cols]], axis=0).astype(F32)
        k2s = pltpu.roll(k2, dh, axis=1)
        v2s = pltpu.roll(v2, dh, axis=1)
        for sub in range(2):
            kvh = 2 * pair + sub
            k_lo, k_hi = (k2, k2s) if sub == 0 else (k2s, k2)
            v_lo, v_hi = (v2, v2s) if sub == 0 else (v2s, v2)
            kz = (jnp.where(lane_kv < dh, k_lo, 0.0).astype(BF16), jnp.where(lane_kv >= dh, k_hi, 0.0).astype(BF16))
            vz = (jnp.where(lane_kv < dh, v_lo, 0.0).astype(BF16), jnp.where(lane_kv >= dh, v_hi, 0.0).astype(BF16))
            qcol0 = kvh * grp * dh
            q_stack = jnp.concatenate(
                [q_ref[:, qcol0 + hp * lanes:qcol0 + (hp + 1) * lanes] for hp in range(pairs)], axis=0)
            acc = None
            inv = [[None, None] for _ in range(pairs)]
            for par in range(2):
                s_all = lax.dot_general(q_stack, kz[par], (((1,), (1,)), ((), ())),
                                        preferred_element_type=F32)
                e_parts = []
                for hp in range(pairs):
                    head = kvh * grp + 2 * hp + par
                    s = s_all[hp * blk:(hp + 1) * blk] + bias_ref[head]
                    sink = sinks_ref[head]
                    m = jnp.maximum(jnp.max(s, axis=-1, keepdims=True), sink)
                    e = jnp.exp(s - m)
                    denom = jnp.sum(e, axis=-1, keepdims=True) + jnp.exp(sink - m)
                    inv[hp][par] = 1.0 / denom
                    e_parts.append(e.astype(BF16))
                pv = jnp.dot(jnp.concatenate(e_parts, axis=0), vz[par], preferred_element_type=F32)
                acc = pv if acc is None else acc + pv
            for hp in range(pairs):
                scale = jnp.where(lane_o < dh, inv[hp][0], inv[hp][1])
                o_ref[:, qcol0 + hp * lanes:qcol0 + (hp + 1) * lanes] = (
                    acc[hp * blk:(hp + 1) * blk] * scale).astype(o_ref.dtype)


def _swa(qkv, sinks):
    s = qkv.shape[0]
    blk = ATT_BLOCK
    dq = ATT_HEADS * ATT_HEAD_DIM
    dkv = ATT_KV_HEADS * ATT_HEAD_DIM
    k_blk = dq // dkv
    v_blk = k_blk + 1
    slopes = 2.0 ** (-8.0 * jnp.arange(1, ATT_HEADS + 1, dtype=F32) / ATT_HEADS)
    prev = lambda n: jnp.maximum(n - 1, 0)
    blocks = [((blk, dq), BF16)] * 2 + [((blk, dkv), BF16)] * 4
    bias = ((ATT_HEADS, blk, 2 * blk), F32)
    return pl.pallas_call(
        _swa_kernel,
        grid=(s // blk,),
        in_specs=[pl.BlockSpec(memory_space=pltpu.SMEM),
                  pl.BlockSpec(memory_space=pltpu.SMEM),
                  pl.BlockSpec((blk, dq), lambda n: (n, 0)),
                  pl.BlockSpec((blk, dkv), lambda n: (prev(n), k_blk)),
                  pl.BlockSpec((blk, dkv), lambda n: (n, k_blk)),
                  pl.BlockSpec((blk, dkv), lambda n: (prev(n), v_blk)),
                  pl.BlockSpec((blk, dkv), lambda n: (n, v_blk))],
        out_specs=pl.BlockSpec((blk, dq), lambda n: (n, 0)),
        out_shape=jax.ShapeDtypeStruct((s, dq), BF16),
        scratch_shapes=[pltpu.VMEM(*bias)],
        compiler_params=_params(1, _vmem_limit(blocks, [bias])),
        name="swa",
    )(slopes, sinks.astype(F32), qkv, qkv, qkv, qkv, qkv)


def kernel(x, norm_mix_g, ret_w_in, ret_w_out, att_w_qkv, att_b_qkv, att_sinks, att_w_out,
           norm_ffn_g, ffn_w_up, ffn_conv_w, ffn_conv_b, ffn_w_down, final_norm_g):
    batch, seq, d = x.shape
    assert batch == 1 and d == D_MODEL
    depth = norm_mix_g.shape[0]

    h = x.reshape(seq, d)
    hn = _rmsnorm(h, norm_mix_g[0])
    for i in range(depth):
        j = i // N_MIXERS
        if i % N_MIXERS == 0:
            decay_table, chunk_decay = _ret_decay_tables()
            proj = _ret_in_proj(hn, ret_w_in, j, decay_table)
            mixed = _retention(proj, chunk_decay)
            w_out = ret_w_out
        else:
            qkv = _matmul(hn, att_w_qkv, j, att_b_qkv, tm=1024, tn=1280)
            mixed = _swa(qkv, att_sinks[j])
            w_out = att_w_out
        h, hn = _proj_res_norm(mixed, w_out, j, h, norm_ffn_g[i], emit_residual=True, tk=512)

        gated = _ffn_up(hn, ffn_w_up, i, ffn_conv_w[i], ffn_conv_b[i])
        if i + 1 < depth:
            h, hn = _proj_res_norm(gated, ffn_w_down, i, h, norm_mix_g[i + 1], emit_residual=True, tk=D_FF // 8)
        else:
            out = _proj_res_norm(gated, ffn_w_down, i, h, final_norm_g, emit_residual=False, tk=D_FF // 8)
    return out.reshape(batch, seq, d)
```

```python
import functools

import jax
import jax.numpy as jnp
from jax import lax
from jax.experimental import pallas as pl
from jax.experimental.pallas import tpu as pltpu

F32 = jnp.float32
BF16 = jnp.bfloat16

D_MODEL = 2048
N_MIXERS = 2

RET_HEADS = 8
RET_QK_DIM = D_MODEL // RET_HEADS
RET_V_DIM = 2 * D_MODEL // RET_HEADS
RET_CHUNK = 256

ATT_HEAD_DIM = 64
ATT_HEADS = D_MODEL // ATT_HEAD_DIM
ATT_KV_HEADS = ATT_HEADS // 8
ATT_GROUP = ATT_HEADS // ATT_KV_HEADS
WINDOW = 128
ATT_BLOCK = 128

D_FF = ((8 * D_MODEL // 3 + 127) // 128) * 128
CONV_WIDTH = 3
EPS = 1e-6

V7X_LANES = 128
V7X_SUBLANES = 8
V7X_VMEM_BYTES = 64 * 1024 * 1024
V7X_COMPILER_SCRATCH_BYTES = 8 * 1024 * 1024
V7X_VMEM_MIN_RESERVE_BYTES = 52 * 1024 * 1024

FF_TILE = 512
D_FF_PAD = ((D_FF + FF_TILE - 1) // FF_TILE) * FF_TILE

D_MODEL_PITCH = D_MODEL + V7X_LANES


def _nbytes(shape, dtype):
    n = 1
    for s in shape:
        n *= s
    return n * jnp.dtype(dtype).itemsize


def _vmem_limit(pipelined, resident=()):
    total = 2 * sum(_nbytes(s, d) for s, d in pipelined)
    total += sum(_nbytes(s, d) for s, d in resident)
    total += V7X_COMPILER_SCRATCH_BYTES
    assert total <= V7X_VMEM_BYTES, total
    return max(total, V7X_VMEM_MIN_RESERVE_BYTES)


def _params(n_grid, vmem_limit):
    return pltpu.CompilerParams(
        dimension_semantics=("arbitrary",) * n_grid, vmem_limit_bytes=vmem_limit)


def _store_padded(ref, rows, value):
    n, d = value.shape
    ref[rows, :d] = value.astype(ref.dtype)
    ref[rows, d:] = jnp.zeros((n, ref.shape[1] - d), ref.dtype)


def _rmsnorm_kernel(x_ref, g_ref, o_ref):
    x = x_ref[...]
    ms = jnp.mean(x * x, axis=-1, keepdims=True)
    _store_padded(o_ref, slice(None), x * lax.rsqrt(ms + EPS) * g_ref[...])


def _rmsnorm(x, gain, *, tm=512):
    m, d = x.shape
    blocks = [((tm, d), F32), ((1, d), F32), ((tm, D_MODEL_PITCH), BF16)]
    return pl.pallas_call(
        _rmsnorm_kernel,
        grid=(m // tm,),
        in_specs=[pl.BlockSpec((tm, d), lambda i: (i, 0)),
                  pl.BlockSpec((1, d), lambda i: (0, 0))],
        out_specs=pl.BlockSpec((tm, D_MODEL_PITCH), lambda i: (i, 0)),
        out_shape=jax.ShapeDtypeStruct((m, D_MODEL_PITCH), BF16),
        compiler_params=_params(1, _vmem_limit(blocks)),
        name="rmsnorm",
    )(x, gain.reshape(1, d))


def _matmul_kernel(*refs, has_bias):
    if has_bias:
        x_ref, w_ref, b_ref, o_ref, wbf_ref = refs
    else:
        x_ref, w_ref, o_ref, wbf_ref = refs

    @pl.when(pl.program_id(1) == 0)
    def _():
        wbf_ref[...] = w_ref[...].astype(BF16)

    k = wbf_ref.shape[0]
    acc = jnp.dot(x_ref[:, :k], wbf_ref[...], preferred_element_type=F32)
    if has_bias:
        acc = acc + b_ref[...]
    o_ref[...] = acc.astype(o_ref.dtype)


def _matmul(x, w, layer, bias=None, *, tm, tn):
    m, pitch = x.shape
    k, n = w.shape[1:]
    has_bias = bias is not None
    in_specs = [pl.BlockSpec((tm, pitch), lambda j, i: (i, 0)),
                pl.BlockSpec((None, k, tn), lambda j, i: (layer, 0, j))]
    args = [x, w]
    blocks = [((tm, pitch), BF16), ((k, tn), F32), ((tm, tn), BF16)]
    if has_bias:
        in_specs.append(pl.BlockSpec((None, 1, tn), lambda j, i: (layer, 0, j)))
        args.append(bias.reshape(bias.shape[0], 1, n))
        blocks.append(((1, tn), F32))
    return pl.pallas_call(
        functools.partial(_matmul_kernel, has_bias=has_bias),
        grid=(n // tn, m // tm),
        in_specs=in_specs,
        out_specs=pl.BlockSpec((tm, tn), lambda j, i: (i, j)),
        out_shape=jax.ShapeDtypeStruct((m, n), BF16),
        scratch_shapes=[pltpu.VMEM((k, tn), BF16)],
        compiler_params=_params(2, _vmem_limit(blocks, [((k, tn), BF16)])),
        name="matmul_bias" if has_bias else "matmul",
    )(*args)


def _ret_decay_tables():
    hh, dk, cs = RET_HEADS, RET_QK_DIM, RET_CHUNK
    log_gamma = jnp.log1p(-(2.0 ** (-5.0 - jnp.arange(hh, dtype=F32))))
    n = jnp.arange(cs, dtype=F32)[:, None]
    q_scale = jnp.exp(log_gamma[None, :] * (n + 1.0 - cs))
    k_scale = jnp.exp(log_gamma[None, :] * (cs - 1.0 - n)) * (dk ** -0.5)
    table = jnp.concatenate([jnp.repeat(q_scale, dk, axis=1), jnp.repeat(k_scale, dk, axis=1)], axis=1)
    chunk_decay = jnp.exp(log_gamma * cs)
    return table, chunk_decay


RET_PROJ_SUB_ROWS = 128


def _ret_in_proj_kernel(x_ref, w_ref, tab_ref, o_ref, wbf_ref, *, n_scaled, n_plain):
    j = pl.program_id(0)
    tm = x_ref.shape[0]
    cs = tab_ref.shape[0]

    @pl.when(pl.program_id(1) == 0)
    def _():
        wbf_ref[...] = w_ref[...].astype(BF16)

    def run(epilogue):
        k = wbf_ref.shape[0]
        for r0 in range(0, tm, RET_PROJ_SUB_ROWS):
            acc = jnp.dot(x_ref[r0:r0 + RET_PROJ_SUB_ROWS, :k], wbf_ref[...], preferred_element_type=F32)
            o_ref[r0:r0 + RET_PROJ_SUB_ROWS, :] = epilogue(acc, r0).astype(o_ref.dtype)

    def scaled(acc, r0):
        t0 = r0 % cs
        return acc * tab_ref[t0:t0 + RET_PROJ_SUB_ROWS, :]

    @pl.when(j < n_scaled)
    def _():
        run(scaled)

    @pl.when((j >= n_scaled) & (j < n_scaled + n_plain))
    def _():
        run(lambda acc, r0: acc)

    @pl.when(j >= n_scaled + n_plain)
    def _():
        run(lambda acc, r0: acc / (1.0 + jnp.exp(-acc)))


def _ret_in_proj(x, w, layer, table, *, tm=1024, tn=1024):
    m, pitch = x.shape
    k, n = w.shape[1:]
    cs, scaled_cols = table.shape
    assert scaled_cols % tn == 0 and cs % RET_PROJ_SUB_ROWS == 0 and tm % cs == 0
    n_scaled = scaled_cols // tn
    n_plain = RET_HEADS * RET_V_DIM // tn
    blocks = [((tm, pitch), BF16), ((k, tn), F32), ((cs, tn), F32), ((tm, tn), BF16)]
    return pl.pallas_call(
        functools.partial(_ret_in_proj_kernel, n_scaled=n_scaled, n_plain=n_plain),
        grid=(n // tn, m // tm),
        in_specs=[pl.BlockSpec((tm, pitch), lambda j, i: (i, 0)),
                  pl.BlockSpec((None, k, tn), lambda j, i: (layer, 0, j)),
                  pl.BlockSpec((cs, tn), lambda j, i: (0, jnp.minimum(j, n_scaled - 1)))],
        out_specs=pl.BlockSpec((tm, tn), lambda j, i: (i, j)),
        out_shape=jax.ShapeDtypeStruct((m, n), BF16),
        scratch_shapes=[pltpu.VMEM((k, tn), BF16)],
        compiler_params=_params(2, _vmem_limit(blocks, [((k, tn), BF16)])),
        name="ret_in_proj",
    )(x, w, table)


def _retention_kernel(cd_ref, q_ref, k_ref, v_ref, g_ref, y_ref, state_ref, sbf_ref):
    hh, dk, dv = RET_HEADS, RET_QK_DIM, RET_V_DIM
    cs = q_ref.shape[0]

    @pl.when(pl.program_id(0) == 0)
    def _():
        state_ref[...] = jnp.zeros_like(state_ref)
        sbf_ref[...] = jnp.zeros_like(sbf_ref)

    causal = (lax.broadcasted_iota(jnp.int32, (cs, cs), 0)
              >= lax.broadcasted_iota(jnp.int32, (cs, cs), 1))
    for h in range(hh):
        q = q_ref[:, h * dk:(h + 1) * dk]
        k = k_ref[:, h * dk:(h + 1) * dk]
        v = v_ref[:, h * dv:(h + 1) * dv]
        s = lax.dot_general(q, k, (((1,), (1,)), ((), ())), preferred_element_type=F32)
        p = jnp.where(causal, s, 0.0).astype(BF16)
        o = (jnp.dot(p, v, preferred_element_type=F32)
             + jnp.dot(q, sbf_ref[h], preferred_element_type=F32))
        upd = lax.dot_general(k, v, (((0,), (0,)), ((), ())), preferred_element_type=F32)
        new_state = (state_ref[h] + upd) * cd_ref[h]
        state_ref[h] = new_state
        sbf_ref[h] = new_state.astype(BF16)

        ms = jnp.mean(o * o, axis=-1, keepdims=True)
        gate = g_ref[:, h * dv:(h + 1) * dv].astype(F32)
        y_ref[:, h * dv:(h + 1) * dv] = (gate * (o * lax.rsqrt(ms + EPS))).astype(y_ref.dtype)


def _retention(proj, chunk_decay):
    s = proj.shape[0]
    hh, dk, dv, cs = RET_HEADS, RET_QK_DIM, RET_V_DIM, RET_CHUNK
    qk_w, v_w = hh * dk, hh * dv
    assert 2 * qk_w == v_w
    blocks = [((cs, qk_w), BF16)] * 2 + [((cs, v_w), BF16)] * 3
    scratch = [((hh, dk, dv), F32), ((hh, dk, dv), BF16)]
    return pl.pallas_call(
        _retention_kernel,
        grid=(s // cs,),
        in_specs=[pl.BlockSpec(memory_space=pltpu.SMEM),
                  pl.BlockSpec((cs, qk_w), lambda c: (c, 0)),
                  pl.BlockSpec((cs, qk_w), lambda c: (c, 1)),
                  pl.BlockSpec((cs, v_w), lambda c: (c, 1)),
                  pl.BlockSpec((cs, v_w), lambda c: (c, 2))],
        out_specs=pl.BlockSpec((cs, v_w), lambda c: (c, 0)),
        out_shape=jax.ShapeDtypeStruct((s, v_w), BF16),
        scratch_shapes=[pltpu.VMEM((hh, dk, dv), F32), pltpu.VMEM((hh, dk, dv), BF16)],
        compiler_params=_params(1, _vmem_limit(blocks, scratch)),
        name="retention",
    )(chunk_decay, proj, proj, proj, proj)


PROJ_SUB_ROWS = 128


def _proj_res_norm_kernel(a_ref, w_ref, res_ref, gain_ref, *rest, emit_residual, n_wchunks):
    out_refs, wbf_ref = rest[:-1], rest[-1]
    step = pl.program_id(0)
    tk = w_ref.shape[0]

    @pl.when(step < n_wchunks)
    def _():
        row0 = pl.multiple_of(step * tk, tk)
        wbf_ref[pl.ds(row0, tk), :] = w_ref[...].astype(BF16)

    @pl.when(step >= n_wchunks)
    def _():
        for r0 in range(0, a_ref.shape[0], PROJ_SUB_ROWS):
            rows = slice(r0, r0 + PROJ_SUB_ROWS)
            h = res_ref[rows, :] + jnp.dot(a_ref[rows, :], wbf_ref[...], preferred_element_type=F32)
            ms = jnp.mean(h * h, axis=-1, keepdims=True)
            hn = h * lax.rsqrt(ms + EPS) * gain_ref[...]
            if emit_residual:
                h_ref, hn_ref = out_refs
                h_ref[rows, :] = h
                _store_padded(hn_ref, rows, hn)
            else:
                (hn_ref,) = out_refs
                hn_ref[rows, :] = hn.astype(hn_ref.dtype)


def _proj_res_norm(a, w, layer, res, gain, *, emit_residual, tk, tm=256):
    m = a.shape[0]
    k, d = w.shape[1:]
    assert k % tk == 0 and tk % (2 * V7X_SUBLANES) == 0
    nw = k // tk
    row = lambda s: (jnp.maximum(s - nw, 0), 0)
    if emit_residual:
        out_shape = (jax.ShapeDtypeStruct((m, d), F32), jax.ShapeDtypeStruct((m, D_MODEL_PITCH), BF16))
        out_specs = (pl.BlockSpec((tm, d), row), pl.BlockSpec((tm, D_MODEL_PITCH), row))
        out_blocks = [((tm, d), F32), ((tm, D_MODEL_PITCH), BF16)]
    else:
        out_shape = jax.ShapeDtypeStruct((m, d), F32)
        out_specs = pl.BlockSpec((tm, d), row)
        out_blocks = [((tm, d), F32)]
    blocks = [((tm, k), BF16), ((tk, d), F32), ((tm, d), F32), ((1, d), F32)] + out_blocks
    return pl.pallas_call(
        functools.partial(_proj_res_norm_kernel, emit_residual=emit_residual, n_wchunks=nw),
        grid=(nw + m // tm,),
        in_specs=[pl.BlockSpec((tm, k), row),
                  pl.BlockSpec((None, tk, d), lambda s: (layer, jnp.minimum(s, nw - 1), 0)),
                  pl.BlockSpec((tm, d), row),
                  pl.BlockSpec((1, d), lambda s: (0, 0))],
        out_specs=out_specs,
        out_shape=out_shape,
        scratch_shapes=[pltpu.VMEM((k, d), BF16)],
        compiler_params=_params(1, _vmem_limit(blocks, [((k, d), BF16)])),
        name="proj_res_norm" if emit_residual else "proj_res_final_norm",
    )(a, w, res, gain.reshape(1, d))


FF_FULL_TILES = D_FF // FF_TILE
FF_REM = D_FF % FF_TILE
FF_NUM_TILES = FF_FULL_TILES + 1
FF_EPI_ROWS = 32


FF_SUB_ROWS = 128


def _ffn_up_kernel(x_ref, wa_ref, wb0_ref, wb1_ref, cw_ref, cb_ref, g_ref, wbf_ref, u0_ref, u1_ref, u2_ref):
    j = pl.program_id(0)
    i = pl.program_id(1)
    tm = x_ref.shape[0]
    tf = wa_ref.shape[1]
    head = wb0_ref.shape[1]
    hist = V7X_SUBLANES

    @pl.when((i == 0) & (j == 0))
    def _():
        u1_ref[tm:, :] = jnp.zeros((hist, 2 * tf), F32)
        u2_ref[tm:, :] = jnp.zeros((hist, 2 * tf), F32)

    @pl.when(i == 0)
    def _():
        wbf_ref[:, :tf] = wa_ref[...].astype(BF16)
        wbf_ref[:, tf:tf + head] = wb0_ref[...].astype(BF16)
        u1_ref[0:hist, :] = jnp.zeros((hist, 2 * tf), F32)
        u2_ref[0:hist, :] = jnp.zeros((hist, 2 * tf), F32)

    @pl.when((i == 0) & (j < FF_FULL_TILES))
    def _():
        wbf_ref[:, tf + head:] = wb1_ref[:, :FF_REM].astype(BF16)

    @pl.when((i == 0) & (j == FF_FULL_TILES))
    def _():
        valid = FF_REM - head
        wbf_ref[:, tf + head:tf + head + valid] = wb1_ref[:, :valid].astype(BF16)
        wbf_ref[:, tf + head + valid:] = jnp.zeros((wbf_ref.shape[0], tf - head - valid), BF16)

    def conv(r0, c0):
        rows = slice(r0, r0 + FF_EPI_ROWS)
        cols = slice(c0, c0 + V7X_LANES)
        sub = V7X_SUBLANES
        tap = [cw_ref[t * sub:(t + 1) * sub, cols][None] for t in range(CONV_WIDTH)]
        piece = lambda ref: ref[rows, cols].reshape(FF_EPI_ROWS // sub, sub, V7X_LANES)
        c = cb_ref[:, cols][None] + (tap[0] * piece(u2_ref) + tap[1] * piece(u1_ref) + tap[2] * piece(u0_ref))
        return c.reshape(FF_EPI_ROWS, V7X_LANES)

    d = wbf_ref.shape[0]
    for s0 in range(0, tm, FF_SUB_ROWS):
        u = jnp.dot(x_ref[s0:s0 + FF_SUB_ROWS, :d], wbf_ref[...], preferred_element_type=F32)
        u0_ref[s0:s0 + FF_SUB_ROWS, :] = u
        u1_ref[s0 + 1:s0 + 1 + FF_SUB_ROWS, :] = u
        u2_ref[s0 + 2:s0 + 2 + FF_SUB_ROWS, :] = u
        for r0 in range(s0, s0 + FF_SUB_ROWS, FF_EPI_ROWS):
            for c0 in range(0, tf, V7X_LANES):
                a = conv(r0, c0)
                b = conv(r0, tf + c0)
                g_ref[r0:r0 + FF_EPI_ROWS, c0:c0 + V7X_LANES] = (
                    a / (1.0 + jnp.exp(-a)) * b).astype(g_ref.dtype)
    u1_ref[0:hist, :] = u1_ref[tm:, :]
    u2_ref[0:hist, :] = u2_ref[tm:, :]


def _ffn_up(x, w_up, layer, conv_w, conv_b, *, tm=1024, tf=FF_TILE):
    m, pitch = x.shape
    d = w_up.shape[1]
    nf = FF_NUM_TILES
    head = tf - FF_REM
    assert tf % head == 0 and FF_REM % head == 0
    sub = V7X_SUBLANES
    def per_tile(t):
        taps = t.shape[0]
        t = jnp.pad(t.reshape(taps, 2, D_FF), ((0, 0), (0, 0), (0, D_FF_PAD - D_FF)))
        t = t.reshape(taps, 2, nf, tf).transpose(2, 0, 1, 3).reshape(nf, taps, 1, 2 * tf)
        return jnp.broadcast_to(t, (nf, taps, sub, 2 * tf)).reshape(nf, taps * sub, 2 * tf)
    cw = per_tile(conv_w)
    cb = per_tile(conv_b.reshape(1, -1))
    blocks = ([((tm, pitch), BF16)] + [((d, tf), F32)] * 2 + [((d, head), F32)]
              + [((CONV_WIDTH * sub, 2 * tf), F32)] + [((sub, 2 * tf), F32)] + [((tm, tf), BF16)])
    scratch = [((d, 2 * tf), BF16), ((tm, 2 * tf), F32)] + [((tm + V7X_SUBLANES, 2 * tf), F32)] * 2
    return pl.pallas_call(
        _ffn_up_kernel,
        grid=(nf, m // tm),
        in_specs=[pl.BlockSpec((tm, pitch), lambda j, i: (i, 0)),
                  pl.BlockSpec((None, d, tf), lambda j, i: (layer, 0, j)),
                  pl.BlockSpec((None, d, head),
                               lambda j, i: (layer, 0, (FF_FULL_TILES + j) * (tf // head) + FF_REM // head)),
                  pl.BlockSpec((None, d, tf), lambda j, i: (layer, 0, FF_FULL_TILES + 1 + j)),
                  pl.BlockSpec((None, CONV_WIDTH * sub, 2 * tf), lambda j, i: (j, 0, 0)),
                  pl.BlockSpec((None, sub, 2 * tf), lambda j, i: (j, 0, 0))],
        out_specs=pl.BlockSpec((tm, tf), lambda j, i: (i, j)),
        out_shape=jax.ShapeDtypeStruct((m, D_FF_PAD), BF16),
        scratch_shapes=[pltpu.VMEM(*shape_dtype) for shape_dtype in scratch],
        compiler_params=_params(2, _vmem_limit(blocks, scratch)),
        name="ffn_up",
    )(x, w_up, w_up, w_up, cw, cb)


def _swa_kernel(slopes_ref, sinks_ref, q_ref, kp_ref, kc_ref, vp_ref, vc_ref, o_ref, bias_ref):
    n = pl.program_id(0)
    blk, dh, grp = ATT_BLOCK, ATT_HEAD_DIM, ATT_GROUP
    lanes = 2 * dh
    pairs = grp // 2

    @pl.when(n <= 1)
    def _():
        qi = lax.broadcasted_iota(jnp.int32, (blk, 2 * blk), 0)
        kj = lax.broadcasted_iota(jnp.int32, (blk, 2 * blk), 1)
        dist = blk + qi - kj
        valid = (dist >= 0) & (dist < WINDOW) & ((n > 0) | (kj >= blk))
        dist_f = dist.astype(F32)
        for head in range(ATT_HEADS):
            bias_ref[head] = jnp.where(valid, -(slopes_ref[head] * dist_f), -jnp.inf)

    lane_kv = lax.broadcasted_iota(jnp.int32, (2 * blk, lanes), 1)
    lane_o = lax.broadcasted_iota(jnp.int32, (blk, lanes), 1)

    for pair in range(ATT_KV_HEADS // 2):
        cols = slice(pair * lanes, (pair + 1) * lanes)
        k2 = jnp.concatenate([kp_ref[:, cols], kc_ref[:, cols]], axis=0).astype(F32) * (dh ** -0.5)
        v2 = jnp.concatenate([vp_ref[:, cols], vc_ref[:, cols]], axis=0).astype(F32)
        k2s = pltpu.roll(k2, dh, axis=1)
        v2s = pltpu.roll(v2, dh, axis=1)
        for sub in range(2):
            kvh = 2 * pair + sub
            k_lo, k_hi = (k2, k2s) if sub == 0 else (k2s, k2)
            v_lo, v_hi = (v2, v2s) if sub == 0 else (v2s, v2)
            kz = (jnp.where(lane_kv < dh, k_lo, 0.0).astype(BF16), jnp.where(lane_kv >= dh, k_hi, 0.0).astype(BF16))
            vz = (jnp.where(lane_kv < dh, v_lo, 0.0).astype(BF16), jnp.where(lane_kv >= dh, v_hi, 0.0).astype(BF16))
            qcol0 = kvh * grp * dh
            q_stack = jnp.concatenate(
                [q_ref[:, qcol0 + hp * lanes:qcol0 + (hp + 1) * lanes] for hp in range(pairs)], axis=0)
            acc = None
            inv = [[None, None] for _ in range(pairs)]
            for par in range(2):
                s_all = lax.dot_general(q_stack, kz[par], (((1,), (1,)), ((), ())),
                                        preferred_element_type=F32)
                e_parts = []
                for hp in range(pairs):
                    head = kvh * grp + 2 * hp + par
                    s = s_all[hp * blk:(hp + 1) * blk] + bias_ref[head]
                    sink = sinks_ref[head]
                    m = jnp.maximum(jnp.max(s, axis=-1, keepdims=True), sink)
                    e = jnp.exp(s - m)
                    denom = jnp.sum(e, axis=-1, keepdims=True) + jnp.exp(sink - m)
                    inv[hp][par] = 1.0 / denom
                    e_parts.append(e.astype(BF16))
                pv = jnp.dot(jnp.concatenate(e_parts, axis=0), vz[par], preferred_element_type=F32)
                acc = pv if acc is None else acc + pv
            for hp in range(pairs):
                scale = jnp.where(lane_o < dh, inv[hp][0], inv[hp][1])
                o_ref[:, qcol0 + hp * lanes:qcol0 + (hp + 1) * lanes] = (
                    acc[hp * blk:(hp + 1) * blk] * scale).astype(o_ref.dtype)


def _swa(qkv, sinks):
    s = qkv.shape[0]
    blk = ATT_BLOCK
    dq = ATT_HEADS * ATT_HEAD_DIM
    dkv = ATT_KV_HEADS * ATT_HEAD_DIM
    k_blk = dq // dkv
    v_blk = k_blk + 1
    slopes = 2.0 ** (-8.0 * jnp.arange(1, ATT_HEADS + 1, dtype=F32) / ATT_HEADS)
    prev = lambda n: jnp.maximum(n - 1, 0)
    blocks = [((blk, dq), BF16)] * 2 + [((blk, dkv), BF16)] * 4
    bias = ((ATT_HEADS, blk, 2 * blk), F32)
    return pl.pallas_call(
        _swa_kernel,
        grid=(s // blk,),
        in_specs=[pl.BlockSpec(memory_space=pltpu.SMEM),
                  pl.BlockSpec(memory_space=pltpu.SMEM),
                  pl.BlockSpec((blk, dq), lambda n: (n, 0)),
                  pl.BlockSpec((blk, dkv), lambda n: (prev(n), k_blk)),
                  pl.BlockSpec((blk, dkv), lambda n: (n, k_blk)),
                  pl.BlockSpec((blk, dkv), lambda n: (prev(n), v_blk)),
                  pl.BlockSpec((blk, dkv), lambda n: (n, v_blk))],
        out_specs=pl.BlockSpec((blk, dq), lambda n: (n, 0)),
        out_shape=jax.ShapeDtypeStruct((s, dq), BF16),
        scratch_shapes=[pltpu.VMEM(*bias)],
        compiler_params=_params(1, _vmem_limit(blocks, [bias])),
        name="swa",
    )(slopes, sinks.astype(F32), qkv, qkv, qkv, qkv, qkv)


def kernel(x, norm_mix_g, ret_w_in, ret_w_out, att_w_qkv, att_b_qkv, att_sinks, att_w_out,
           norm_ffn_g, ffn_w_up, ffn_conv_w, ffn_conv_b, ffn_w_down, final_norm_g):
    batch, seq, d = x.shape
    assert batch == 1 and d == D_MODEL
    depth = norm_mix_g.shape[0]

    h = x.reshape(seq, d)
    hn = _rmsnorm(h, norm_mix_g[0])
    for i in range(depth):
        j = i // N_MIXERS
        if i % N_MIXERS == 0:
            decay_table, chunk_decay = _ret_decay_tables()
            proj = _ret_in_proj(hn, ret_w_in, j, decay_table)
            mixed = _retention(proj, chunk_decay)
            w_out = ret_w_out
        else:
            qkv = _matmul(hn, att_w_qkv, j, att_b_qkv, tm=1024, tn=1280)
            mixed = _swa(qkv, att_sinks[j])
            w_out = att_w_out
        h, hn = _proj_res_norm(mixed, w_out, j, h, norm_ffn_g[i], emit_residual=True, tk=512)

        gated = _ffn_up(hn, ffn_w_up, i, ffn_conv_w[i], ffn_conv_b[i])
        if i + 1 < depth:
            h, hn = _proj_res_norm(gated, ffn_w_down, i, h, norm_mix_g[i + 1], emit_residual=True, tk=D_FF // 8)
        else:
            out = _proj_res_norm(gated, ffn_w_down, i, h, final_norm_g, emit_residual=False, tk=D_FF // 8)
    return out.reshape(batch, seq, d)
```

```python
import functools

import jax
import jax.numpy as jnp
from jax import lax
from jax.experimental import pallas as pl
from jax.experimental.pallas import tpu as pltpu

F32 = jnp.float32
BF16 = jnp.bfloat16

D_MODEL = 2048
N_MIXERS = 2

RET_HEADS = 8
RET_QK_DIM = D_MODEL // RET_HEADS
RET_V_DIM = 2 * D_MODEL // RET_HEADS
RET_CHUNK = 256

ATT_HEAD_DIM = 64
ATT_HEADS = D_MODEL // ATT_HEAD_DIM
ATT_KV_HEADS = ATT_HEADS // 8
ATT_GROUP = ATT_HEADS // ATT_KV_HEADS
WINDOW = 128
ATT_BLOCK = 128

D_FF = ((8 * D_MODEL // 3 + 127) // 128) * 128
CONV_WIDTH = 3
EPS = 1e-6

V7X_LANES = 128
V7X_SUBLANES = 8
V7X_VMEM_BYTES = 64 * 1024 * 1024
V7X_COMPILER_SCRATCH_BYTES = 8 * 1024 * 1024
V7X_VMEM_MIN_RESERVE_BYTES = 52 * 1024 * 1024

FF_TILE = 512
D_FF_PAD = ((D_FF + FF_TILE - 1) // FF_TILE) * FF_TILE

D_MODEL_PITCH = D_MODEL + V7X_LANES


def _nbytes(shape, dtype):
    n = 1
    for s in shape:
        n *= s
    return n * jnp.dtype(dtype).itemsize


def _vmem_limit(pipelined, resident=()):
    total = 2 * sum(_nbytes(s, d) for s, d in pipelined)
    total += sum(_nbytes(s, d) for s, d in resident)
    total += V7X_COMPILER_SCRATCH_BYTES
    assert total <= V7X_VMEM_BYTES, total
    return max(total, V7X_VMEM_MIN_RESERVE_BYTES)


def _params(n_grid, vmem_limit):
    return pltpu.CompilerParams(
        dimension_semantics=("arbitrary",) * n_grid, vmem_limit_bytes=vmem_limit)


def _store_padded(ref, rows, value):
    n, d = value.shape
    ref[rows, :d] = value.astype(ref.dtype)
    ref[rows, d:] = jnp.zeros((n, ref.shape[1] - d), ref.dtype)


def _rmsnorm_kernel(x_ref, g_ref, o_ref):
    x = x_ref[...]
    ms = jnp.mean(x * x, axis=-1, keepdims=True)
    _store_padded(o_ref, slice(None), x * lax.rsqrt(ms + EPS) * g_ref[...])


def _rmsnorm(x, gain, *, tm=1024):
    m, d = x.shape
    blocks = [((tm, d), F32), ((1, d), F32), ((tm, D_MODEL_PITCH), BF16)]
    return pl.pallas_call(
        _rmsnorm_kernel,
        grid=(m // tm,),
        in_specs=[pl.BlockSpec((tm, d), lambda i: (i, 0)),
                  pl.BlockSpec((1, d), lambda i: (0, 0))],
        out_specs=pl.BlockSpec((tm, D_MODEL_PITCH), lambda i: (i, 0)),
        out_shape=jax.ShapeDtypeStruct((m, D_MODEL_PITCH), BF16),
        compiler_params=_params(1, _vmem_limit(blocks)),
        name="rmsnorm",
    )(x, gain.reshape(1, d))


def _matmul_kernel(*refs, has_bias):
    if has_bias:
        x_ref, w_ref, b_ref, o_ref, wbf_ref = refs
    else:
        x_ref, w_ref, o_ref, wbf_ref = refs

    @pl.when(pl.program_id(1) == 0)
    def _():
        wbf_ref[...] = w_ref[...].astype(BF16)

    k = wbf_ref.shape[0]
    acc = jnp.dot(x_ref[:, :k], wbf_ref[...], preferred_element_type=F32)
    if has_bias:
        acc = acc + b_ref[...]
    o_ref[...] = acc.astype(o_ref.dtype)


def _matmul(x, w, layer, bias=None, *, tm, tn):
    m, pitch = x.shape
    k, n = w.shape[1:]
    has_bias = bias is not None
    in_specs = [pl.BlockSpec((tm, pitch), lambda j, i: (i, 0)),
                pl.BlockSpec((None, k, tn), lambda j, i: (layer, 0, j))]
    args = [x, w]
    blocks = [((tm, pitch), BF16), ((k, tn), F32), ((tm, tn), BF16)]
    if has_bias:
        in_specs.append(pl.BlockSpec((None, 1, tn), lambda j, i: (layer, 0, j)))
        args.append(bias.reshape(bias.shape[0], 1, n))
        blocks.append(((1, tn), F32))
    return pl.pallas_call(
        functools.partial(_matmul_kernel, has_bias=has_bias),
        grid=(n // tn, m // tm),
        in_specs=in_specs,
        out_specs=pl.BlockSpec((tm, tn), lambda j, i: (i, j)),
        out_shape=jax.ShapeDtypeStruct((m, n), BF16),
        scratch_shapes=[pltpu.VMEM((k, tn), BF16)],
        compiler_params=_params(2, _vmem_limit(blocks, [((k, tn), BF16)])),
        name="matmul_bias" if has_bias else "matmul",
    )(*args)


def _ret_decay_tables():
    hh, dk, cs = RET_HEADS, RET_QK_DIM, RET_CHUNK
    log_gamma = jnp.log1p(-(2.0 ** (-5.0 - jnp.arange(hh, dtype=F32))))
    n = jnp.arange(cs, dtype=F32)[:, None]
    q_scale = jnp.exp(log_gamma[None, :] * (n + 1.0 - cs))
    k_scale = jnp.exp(log_gamma[None, :] * (cs - 1.0 - n)) * (dk ** -0.5)
    table = jnp.concatenate([jnp.repeat(q_scale, dk, axis=1), jnp.repeat(k_scale, dk, axis=1)], axis=1)
    chunk_decay = jnp.exp(log_gamma * cs)
    return table, chunk_decay


RET_PROJ_SUB_ROWS = 256


def _ret_in_proj_kernel(x_ref, w_ref, tab_ref, o_ref, wbf_ref, *, n_scaled, n_plain):
    j = pl.program_id(0)
    tm = x_ref.shape[0]
    cs = tab_ref.shape[0]

    @pl.when(pl.program_id(1) == 0)
    def _():
        wbf_ref[...] = w_ref[...].astype(BF16)

    def run(epilogue):
        w = wbf_ref[...]
        for r0 in range(0, tm, RET_PROJ_SUB_ROWS):
            acc = jnp.dot(x_ref[r0:r0 + RET_PROJ_SUB_ROWS, :w.shape[0]], w, preferred_element_type=F32)
            o_ref[r0:r0 + RET_PROJ_SUB_ROWS, :] = epilogue(acc, r0).astype(o_ref.dtype)

    def scaled(acc, r0):
        t0 = r0 % cs
        return acc * tab_ref[t0:t0 + RET_PROJ_SUB_ROWS, :]

    @pl.when(j < n_scaled)
    def _():
        run(scaled)

    @pl.when((j >= n_scaled) & (j < n_scaled + n_plain))
    def _():
        run(lambda acc, r0: acc)

    @pl.when(j >= n_scaled + n_plain)
    def _():
        run(lambda acc, r0: acc / (1.0 + jnp.exp(-acc)))


def _ret_in_proj(x, w, layer, table, *, tm=2048, tn=1024):
    m, pitch = x.shape
    k, n = w.shape[1:]
    cs, scaled_cols = table.shape
    assert scaled_cols % tn == 0 and cs % RET_PROJ_SUB_ROWS == 0 and tm % cs == 0
    n_scaled = scaled_cols // tn
    n_plain = RET_HEADS * RET_V_DIM // tn
    blocks = [((tm, pitch), BF16), ((k, tn), F32), ((cs, tn), F32), ((tm, tn), BF16)]
    return pl.pallas_call(
        functools.partial(_ret_in_proj_kernel, n_scaled=n_scaled, n_plain=n_plain),
        grid=(n // tn, m // tm),
        in_specs=[pl.BlockSpec((tm, pitch), lambda j, i: (i, 0)),
                  pl.BlockSpec((None, k, tn), lambda j, i: (layer, 0, j)),
                  pl.BlockSpec((cs, tn), lambda j, i: (0, jnp.minimum(j, n_scaled - 1)))],
        out_specs=pl.BlockSpec((tm, tn), lambda j, i: (i, j)),
        out_shape=jax.ShapeDtypeStruct((m, n), BF16),
        scratch_shapes=[pltpu.VMEM((k, tn), BF16)],
        compiler_params=_params(2, _vmem_limit(blocks, [((k, tn), BF16)])),
        name="ret_in_proj",
    )(x, w, table)


def _retention_kernel(cd_ref, q_ref, k_ref, v_ref, g_ref, y_ref, state_ref, sbf_ref):
    hh, dk, dv = RET_HEADS, RET_QK_DIM, RET_V_DIM
    cs = q_ref.shape[0]

    @pl.when(pl.program_id(0) == 0)
    def _():
        state_ref[...] = jnp.zeros_like(state_ref)
        sbf_ref[...] = jnp.zeros_like(sbf_ref)

    causal = (lax.broadcasted_iota(jnp.int32, (cs, cs), 0)
              >= lax.broadcasted_iota(jnp.int32, (cs, cs), 1))
    for h in range(hh):
        q = q_ref[:, h * dk:(h + 1) * dk]
        k = k_ref[:, h * dk:(h + 1) * dk]
        v = v_ref[:, h * dv:(h + 1) * dv]
        s = lax.dot_general(q, k, (((1,), (1,)), ((), ())), preferred_element_type=F32)
        p = jnp.where(causal, s, 0.0).astype(BF16)
        o = (jnp.dot(p, v, preferred_element_type=F32)
             + jnp.dot(q, sbf_ref[h], preferred_element_type=F32))
        upd = lax.dot_general(k, v, (((0,), (0,)), ((), ())), preferred_element_type=F32)
        new_state = (state_ref[h] + upd) * cd_ref[h]
        state_ref[h] = new_state
        sbf_ref[h] = new_state.astype(BF16)

        ms = jnp.mean(o * o, axis=-1, keepdims=True)
        gate = g_ref[:, h * dv:(h + 1) * dv].astype(F32)
        y_ref[:, h * dv:(h + 1) * dv] = (gate * (o * lax.rsqrt(ms + EPS))).astype(y_ref.dtype)


def _retention(proj, chunk_decay):
    s = proj.shape[0]
    hh, dk, dv, cs = RET_HEADS, RET_QK_DIM, RET_V_DIM, RET_CHUNK
    qk_w, v_w = hh * dk, hh * dv
    assert 2 * qk_w == v_w
    blocks = [((cs, qk_w), BF16)] * 2 + [((cs, v_w), BF16)] * 3
    scratch = [((hh, dk, dv), F32), ((hh, dk, dv), BF16)]
    return pl.pallas_call(
        _retention_kernel,
        grid=(s // cs,),
        in_specs=[pl.BlockSpec(memory_space=pltpu.SMEM),
                  pl.BlockSpec((cs, qk_w), lambda c: (c, 0)),
                  pl.BlockSpec((cs, qk_w), lambda c: (c, 1)),
                  pl.BlockSpec((cs, v_w), lambda c: (c, 1)),
                  pl.BlockSpec((cs, v_w), lambda c: (c, 2))],
        out_specs=pl.BlockSpec((cs, v_w), lambda c: (c, 0)),
        out_shape=jax.ShapeDtypeStruct((s, v_w), BF16),
        scratch_shapes=[pltpu.VMEM((hh, dk, dv), F32), pltpu.VMEM((hh, dk, dv), BF16)],
        compiler_params=_params(1, _vmem_limit(blocks, scratch)),
        name="retention",
    )(chunk_decay, proj, proj, proj, proj)


def _proj_res_norm_kernel(a_ref, w_ref, res_ref, gain_ref, *rest, emit_residual, n_wchunks):
    out_refs, wbf_ref = rest[:-1], rest[-1]
    step = pl.program_id(0)
    tk = w_ref.shape[0]

    @pl.when(step < n_wchunks)
    def _():
        row0 = pl.multiple_of(step * tk, tk)
        wbf_ref[pl.ds(row0, tk), :] = w_ref[...].astype(BF16)

    @pl.when(step >= n_wchunks)
    def _():
        h = res_ref[...] + jnp.dot(a_ref[...], wbf_ref[...], preferred_element_type=F32)
        ms = jnp.mean(h * h, axis=-1, keepdims=True)
        hn = h * lax.rsqrt(ms + EPS) * gain_ref[...]
        if emit_residual:
            h_ref, hn_ref = out_refs
            h_ref[...] = h
            _store_padded(hn_ref, slice(None), hn)
        else:
            (hn_ref,) = out_refs
            hn_ref[...] = hn.astype(hn_ref.dtype)


def _proj_res_norm(a, w, layer, res, gain, *, emit_residual, tk, tm=256):
    m = a.shape[0]
    k, d = w.shape[1:]
    assert k % tk == 0 and tk % (2 * V7X_SUBLANES) == 0
    nw = k // tk
    row = lambda s: (jnp.maximum(s - nw, 0), 0)
    if emit_residual:
        out_shape = (jax.ShapeDtypeStruct((m, d), F32), jax.ShapeDtypeStruct((m, D_MODEL_PITCH), BF16))
        out_specs = (pl.BlockSpec((tm, d), row), pl.BlockSpec((tm, D_MODEL_PITCH), row))
        out_blocks = [((tm, d), F32), ((tm, D_MODEL_PITCH), BF16)]
    else:
        out_shape = jax.ShapeDtypeStruct((m, d), F32)
        out_specs = pl.BlockSpec((tm, d), row)
        out_blocks = [((tm, d), F32)]
    blocks = [((tm, k), BF16), ((tk, d), F32), ((tm, d), F32), ((1, d), F32)] + out_blocks
    return pl.pallas_call(
        functools.partial(_proj_res_norm_kernel, emit_residual=emit_residual, n_wchunks=nw),
        grid=(nw + m // tm,),
        in_specs=[pl.BlockSpec((tm, k), row),
                  pl.BlockSpec((None, tk, d), lambda s: (layer, jnp.minimum(s, nw - 1), 0)),
                  pl.BlockSpec((tm, d), row),
                  pl.BlockSpec((1, d), lambda s: (0, 0))],
        out_specs=out_specs,
        out_shape=out_shape,
        scratch_shapes=[pltpu.VMEM((k, d), BF16)],
        compiler_params=_params(1, _vmem_limit(blocks, [((k, d), BF16)])),
        name="proj_res_norm" if emit_residual else "proj_res_final_norm",
    )(a, w, res, gain.reshape(1, d))


FF_FULL_TILES = D_FF // FF_TILE
FF_REM = D_FF % FF_TILE
FF_NUM_TILES = FF_FULL_TILES + 1
FF_EPI_ROWS = 32


FF_SUB_ROWS = 256


def _ffn_up_kernel(x_ref, wa_ref, wb0_ref, wb1_ref, cw_ref, cb_ref, g_ref, wbf_ref, *stage_refs):
    j = pl.program_id(0)
    i = pl.program_id(1)
    tm = x_ref.shape[0]
    tf = wa_ref.shape[1]
    head = wb0_ref.shape[1]
    hist = V7X_SUBLANES
    sub_rows = FF_SUB_ROWS
    n_sub = tm // sub_rows
    assert n_sub % 2 == 0
    slots = (stage_refs[0:3], stage_refs[3:6])
    last_u1, last_u2 = slots[1][1], slots[1][2]

    @pl.when((i == 0) & (j == 0))
    def _():
        for _, u1_ref, u2_ref in slots:
            u1_ref[sub_rows:, :] = jnp.zeros((hist, 2 * tf), F32)
            u2_ref[sub_rows:, :] = jnp.zeros((hist, 2 * tf), F32)

    @pl.when(i == 0)
    def _():
        wbf_ref[:, :tf] = wa_ref[...].astype(BF16)
        wbf_ref[:, tf:tf + head] = wb0_ref[...].astype(BF16)
        last_u1[sub_rows:, :] = jnp.zeros((hist, 2 * tf), F32)
        last_u2[sub_rows:, :] = jnp.zeros((hist, 2 * tf), F32)

    @pl.when((i == 0) & (j < FF_FULL_TILES))
    def _():
        wbf_ref[:, tf + head:] = wb1_ref[:, :FF_REM].astype(BF16)

    @pl.when((i == 0) & (j == FF_FULL_TILES))
    def _():
        valid = FF_REM - head
        wbf_ref[:, tf + head:tf + head + valid] = wb1_ref[:, :valid].astype(BF16)
        wbf_ref[:, tf + head + valid:] = jnp.zeros((wbf_ref.shape[0], tf - head - valid), BF16)

    def conv(slot, r0, c0):
        u0_ref, u1_ref, u2_ref = slot
        rows = slice(r0, r0 + FF_EPI_ROWS)
        cols = slice(c0, c0 + V7X_LANES)
        sub = V7X_SUBLANES
        tap = [cw_ref[t * sub:(t + 1) * sub, cols][None] for t in range(CONV_WIDTH)]
        piece = lambda ref: ref[rows, cols].reshape(FF_EPI_ROWS // sub, sub, V7X_LANES)
        c = cb_ref[:, cols][None] + (tap[0] * piece(u2_ref) + tap[1] * piece(u1_ref) + tap[2] * piece(u0_ref))
        return c.reshape(FF_EPI_ROWS, V7X_LANES)

    w = wbf_ref[...]
    for k in range(n_sub):
        s0 = k * sub_rows
        slot = slots[k % 2]
        u0_ref, u1_ref, u2_ref = slot
        _, prev_u1, prev_u2 = slots[(k + 1) % 2]
        u = jnp.dot(x_ref[s0:s0 + sub_rows, :w.shape[0]], w, preferred_element_type=F32)
        u1_ref[0:hist, :] = prev_u1[sub_rows:, :]
        u2_ref[0:hist, :] = prev_u2[sub_rows:, :]
        u0_ref[...] = u
        u1_ref[1:1 + sub_rows, :] = u
        u2_ref[2:2 + sub_rows, :] = u
        for r0 in range(0, sub_rows, FF_EPI_ROWS):
            for c0 in range(0, tf, V7X_LANES):
                a = conv(slot, r0, c0)
                b = conv(slot, r0, tf + c0)
                g_ref[s0 + r0:s0 + r0 + FF_EPI_ROWS, c0:c0 + V7X_LANES] = (
                    a / (1.0 + jnp.exp(-a)) * b).astype(g_ref.dtype)


def _ffn_up(x, w_up, layer, conv_w, conv_b, *, tm=2048, tf=FF_TILE):
    m, pitch = x.shape
    d = w_up.shape[1]
    nf = FF_NUM_TILES
    head = tf - FF_REM
    assert tf % head == 0 and FF_REM % head == 0
    sub = V7X_SUBLANES
    def per_tile(t):
        taps = t.shape[0]
        t = jnp.pad(t.reshape(taps, 2, D_FF), ((0, 0), (0, 0), (0, D_FF_PAD - D_FF)))
        t = t.reshape(taps, 2, nf, tf).transpose(2, 0, 1, 3).reshape(nf, taps, 1, 2 * tf)
        return jnp.broadcast_to(t, (nf, taps, sub, 2 * tf)).reshape(nf, taps * sub, 2 * tf)
    cw = per_tile(conv_w)
    cb = per_tile(conv_b.reshape(1, -1))
    blocks = ([((tm, pitch), BF16)] + [((d, tf), F32)] * 2 + [((d, head), F32)]
              + [((CONV_WIDTH * sub, 2 * tf), F32)] + [((sub, 2 * tf), F32)] + [((tm, tf), BF16)])
    slot = [((FF_SUB_ROWS, 2 * tf), F32)] + [((FF_SUB_ROWS + V7X_SUBLANES, 2 * tf), F32)] * 2
    scratch = [((d, 2 * tf), BF16)] + slot * 2
    return pl.pallas_call(
        _ffn_up_kernel,
        grid=(nf, m // tm),
        in_specs=[pl.BlockSpec((tm, pitch), lambda j, i: (i, 0)),
                  pl.BlockSpec((None, d, tf), lambda j, i: (layer, 0, j)),
                  pl.BlockSpec((None, d, head),
                               lambda j, i: (layer, 0, (FF_FULL_TILES + j) * (tf // head) + FF_REM // head)),
                  pl.BlockSpec((None, d, tf), lambda j, i: (layer, 0, FF_FULL_TILES + 1 + j)),
                  pl.BlockSpec((None, CONV_WIDTH * sub, 2 * tf), lambda j, i: (j, 0, 0)),
                  pl.BlockSpec((None, sub, 2 * tf), lambda j, i: (j, 0, 0))],
        out_specs=pl.BlockSpec((tm, tf), lambda j, i: (i, j)),
        out_shape=jax.ShapeDtypeStruct((m, D_FF_PAD), BF16),
        scratch_shapes=[pltpu.VMEM(*shape_dtype) for shape_dtype in scratch],
        compiler_params=_params(2, _vmem_limit(blocks, scratch)),
        name="ffn_up",
    )(x, w_up, w_up, w_up, cw, cb)


def _swa_kernel(slopes_ref, sinks_ref, q_ref, kp_ref, kc_ref, vp_ref, vc_ref, o_ref, bias_ref):
    n = pl.program_id(0)
    blk, dh, grp = ATT_BLOCK, ATT_HEAD_DIM, ATT_GROUP
    lanes = 2 * dh
    pairs = grp // 2

    @pl.when(n <= 1)
    def _():
        qi = lax.broadcasted_iota(jnp.int32, (blk, 2 * blk), 0)
        kj = lax.broadcasted_iota(jnp.int32, (blk, 2 * blk), 1)
        dist = blk + qi - kj
        valid = (dist >= 0) & (dist < WINDOW) & ((n > 0) | (kj >= blk))
        dist_f = dist.astype(F32)
        for head in range(ATT_HEADS):
            bias_ref[head] = jnp.where(valid, -(slopes_ref[head] * dist_f), -jnp.inf)

    lane_kv = lax.broadcasted_iota(jnp.int32, (2 * blk, lanes), 1)
    lane_o = lax.broadcasted_iota(jnp.int32, (blk, lanes), 1)

    for pair in range(ATT_KV_HEADS // 2):
        cols = slice(pair * lanes, (pair + 1) * lanes)
        k2 = jnp.concatenate([kp_ref[:, cols], kc_ref[:, cols]], axis=0).astype(F32) * (dh ** -0.5)
        v2 = jnp.concatenate([vp_ref[:, cols], vc_ref[:, cols]], axis=0).astype(F32)
        k2s = pltpu.roll(k2, dh, axis=1)
        v2s = pltpu.roll(v2, dh, axis=1)
        for sub in range(2):
            kvh = 2 * pair + sub
            k_lo, k_hi = (k2, k2s) if sub == 0 else (k2s, k2)
            v_lo, v_hi = (v2, v2s) if sub == 0 else (v2s, v2)
            kz = (jnp.where(lane_kv < dh, k_lo, 0.0).astype(BF16), jnp.where(lane_kv >= dh, k_hi, 0.0).astype(BF16))
            vz = (jnp.where(lane_kv < dh, v_lo, 0.0).astype(BF16), jnp.where(lane_kv >= dh, v_hi, 0.0).astype(BF16))
            qcol0 = kvh * grp * dh
            q_stack = jnp.concatenate(
                [q_ref[:, qcol0 + hp * lanes:qcol0 + (hp + 1) * lanes] for hp in range(pairs)], axis=0)
            acc = None
            inv = [[None, None] for _ in range(pairs)]
            for par in range(2):
                s_all = lax.dot_general(q_stack, kz[par], (((1,), (1,)), ((), ())),
                                        preferred_element_type=F32)
                e_parts = []
                for hp in range(pairs):
                    head = kvh * grp + 2 * hp + par
                    s = s_all[hp * blk:(hp + 1) * blk] + bias_ref[head]
                    sink = sinks_ref[head]
                    m = jnp.maximum(jnp.max(s, axis=-1, keepdims=True), sink)
                    e = jnp.exp(s - m)
                    denom = jnp.sum(e, axis=-1, keepdims=True) + jnp.exp(sink - m)
                    inv[hp][par] = 1.0 / denom
                    e_parts.append(e.astype(BF16))
                pv = jnp.dot(jnp.concatenate(e_parts, axis=0), vz[par], preferred_element_type=F32)
                acc = pv if acc is None else acc + pv
            for hp in range(pairs):
                scale = jnp.where(lane_o < dh, inv[hp][0], inv[hp][1])
                o_ref[:, qcol0 + hp * lanes:qcol0 + (hp + 1) * lanes] = (
                    acc[hp * blk:(hp + 1) * blk] * scale).astype(o_ref.dtype)


def _swa(qkv, sinks):
    s = qkv.shape[0]
    blk = ATT_BLOCK
    dq = ATT_HEADS * ATT_HEAD_DIM
    dkv = ATT_KV_HEADS * ATT_HEAD_DIM
    k_blk = dq // dkv
    v_blk = k_blk + 1
    slopes = 2.0 ** (-8.0 * jnp.arange(1, ATT_HEADS + 1, dtype=F32) / ATT_HEADS)
    prev = lambda n: jnp.maximum(n - 1, 0)
    blocks = [((blk, dq), BF16)] * 2 + [((blk, dkv), BF16)] * 4
    bias = ((ATT_HEADS, blk, 2 * blk), F32)
    return pl.pallas_call(
        _swa_kernel,
        grid=(s // blk,),
        in_specs=[pl.BlockSpec(memory_space=pltpu.SMEM),
                  pl.BlockSpec(memory_space=pltpu.SMEM),
                  pl.BlockSpec((blk, dq), lambda n: (n, 0)),
                  pl.BlockSpec((blk, dkv), lambda n: (prev(n), k_blk)),
                  pl.BlockSpec((blk, dkv), lambda n: (n, k_blk)),
                  pl.BlockSpec((blk, dkv), lambda n: (prev(n), v_blk)),
                  pl.BlockSpec((blk, dkv), lambda n: (n, v_blk))],
        out_specs=pl.BlockSpec((blk, dq), lambda n: (n, 0)),
        out_shape=jax.ShapeDtypeStruct((s, dq), BF16),
        scratch_shapes=[pltpu.VMEM(*bias)],
        compiler_params=_params(1, _vmem_limit(blocks, [bias])),
        name="swa",
    )(slopes, sinks.astype(F32), qkv, qkv, qkv, qkv, qkv)


def kernel(x, norm_mix_g, ret_w_in, ret_w_out, att_w_qkv, att_b_qkv, att_sinks, att_w_out,
           norm_ffn_g, ffn_w_up, ffn_conv_w, ffn_conv_b, ffn_w_down, final_norm_g):
    batch, seq, d = x.shape
    assert batch == 1 and d == D_MODEL
    depth = norm_mix_g.shape[0]

    h = x.reshape(seq, d)
    hn = _rmsnorm(h, norm_mix_g[0])
    for i in range(depth):
        j = i // N_MIXERS
        if i % N_MIXERS == 0:
            decay_table, chunk_decay = _ret_decay_tables()
            proj = _ret_in_proj(hn, ret_w_in, j, decay_table)
            mixed = _retention(proj, chunk_decay)
            w_out = ret_w_out
        else:
            qkv = _matmul(hn, att_w_qkv, j, att_b_qkv, tm=1024, tn=1280)
            mixed = _swa(qkv, att_sinks[j])
            w_out = att_w_out
        h, hn = _proj_res_norm(mixed, w_out, j, h, norm_ffn_g[i], emit_residual=True, tk=512, tm=512)

        gated = _ffn_up(hn, ffn_w_up, i, ffn_conv_w[i], ffn_conv_b[i])
        if i + 1 < depth:
            h, hn = _proj_res_norm(gated, ffn_w_down, i, h, norm_mix_g[i + 1], emit_residual=True, tk=D_FF // 8)
        else:
            out = _proj_res_norm(gated, ffn_w_down, i, h, final_norm_g, emit_residual=False, tk=D_FF // 8)
    return out.reshape(batch, seq, d)
```

```python
import functools

import jax
import jax.numpy as jnp
from jax import lax
from jax.experimental import pallas as pl
from jax.experimental.pallas import tpu as pltpu

F32 = jnp.float32
BF16 = jnp.bfloat16

D_MODEL = 2048
N_MIXERS = 2

RET_HEADS = 8
RET_QK_DIM = D_MODEL // RET_HEADS
RET_V_DIM = 2 * D_MODEL // RET_HEADS
RET_CHUNK = 256

ATT_HEAD_DIM = 64
ATT_HEADS = D_MODEL // ATT_HEAD_DIM
ATT_KV_HEADS = ATT_HEADS // 8
ATT_GROUP = ATT_HEADS // ATT_KV_HEADS
WINDOW = 128
ATT_BLOCK = 128

D_FF = ((8 * D_MODEL // 3 + 127) // 128) * 128
CONV_WIDTH = 3
EPS = 1e-6

V7X_LANES = 128
V7X_SUBLANES = 8
V7X_VMEM_BYTES = 64 * 1024 * 1024
V7X_COMPILER_SCRATCH_BYTES = 8 * 1024 * 1024
V7X_VMEM_MIN_RESERVE_BYTES = 52 * 1024 * 1024

FF_TILE = 512
D_FF_PAD = ((D_FF + FF_TILE - 1) // FF_TILE) * FF_TILE

D_MODEL_PITCH = D_MODEL + V7X_LANES


def _nbytes(shape, dtype):
    n = 1
    for s in shape:
        n *= s
    return n * jnp.dtype(dtype).itemsize


def _vmem_limit(pipelined, resident=()):
    total = 2 * sum(_nbytes(s, d) for s, d in pipelined)
    total += sum(_nbytes(s, d) for s, d in resident)
    total += V7X_COMPILER_SCRATCH_BYTES
    assert total <= V7X_VMEM_BYTES, total
    return max(total, V7X_VMEM_MIN_RESERVE_BYTES)


def _params(n_grid, vmem_limit):
    return pltpu.CompilerParams(
        dimension_semantics=("arbitrary",) * n_grid, vmem_limit_bytes=vmem_limit)


def _store_padded(ref, rows, value):
    n, d = value.shape
    ref[rows, :d] = value.astype(ref.dtype)
    ref[rows, d:] = jnp.zeros((n, ref.shape[1] - d), ref.dtype)


def _rmsnorm_kernel(x_ref, g_ref, o_ref):
    x = x_ref[...]
    ms = jnp.mean(x * x, axis=-1, keepdims=True)
    _store_padded(o_ref, slice(None), x * lax.rsqrt(ms + EPS) * g_ref[...])


def _rmsnorm(x, gain, *, tm=1024):
    m, d = x.shape
    blocks = [((tm, d), F32), ((1, d), F32), ((tm, D_MODEL_PITCH), BF16)]
    return pl.pallas_call(
        _rmsnorm_kernel,
        grid=(m // tm,),
        in_specs=[pl.BlockSpec((tm, d), lambda i: (i, 0)),
                  pl.BlockSpec((1, d), lambda i: (0, 0))],
        out_specs=pl.BlockSpec((tm, D_MODEL_PITCH), lambda i: (i, 0)),
        out_shape=jax.ShapeDtypeStruct((m, D_MODEL_PITCH), BF16),
        compiler_params=_params(1, _vmem_limit(blocks)),
        name="rmsnorm",
    )(x, gain.reshape(1, d))


def _matmul_kernel(*refs, has_bias):
    if has_bias:
        x_ref, w_ref, b_ref, o_ref, wbf_ref = refs
    else:
        x_ref, w_ref, o_ref, wbf_ref = refs

    @pl.when(pl.program_id(1) == 0)
    def _():
        wbf_ref[...] = w_ref[...].astype(BF16)

    k = wbf_ref.shape[0]
    acc = jnp.dot(x_ref[:, :k], wbf_ref[...], preferred_element_type=F32)
    if has_bias:
        acc = acc + b_ref[...]
    o_ref[...] = acc.astype(o_ref.dtype)


def _matmul(x, w, layer, bias=None, *, tm, tn):
    m, pitch = x.shape
    k, n = w.shape[1:]
    has_bias = bias is not None
    in_specs = [pl.BlockSpec((tm, pitch), lambda j, i: (i, 0)),
                pl.BlockSpec((None, k, tn), lambda j, i: (layer, 0, j))]
    args = [x, w]
    blocks = [((tm, pitch), BF16), ((k, tn), F32), ((tm, tn), BF16)]
    if has_bias:
        in_specs.append(pl.BlockSpec((None, 1, tn), lambda j, i: (layer, 0, j)))
        args.append(bias.reshape(bias.shape[0], 1, n))
        blocks.append(((1, tn), F32))
    return pl.pallas_call(
        functools.partial(_matmul_kernel, has_bias=has_bias),
        grid=(n // tn, m // tm),
        in_specs=in_specs,
        out_specs=pl.BlockSpec((tm, tn), lambda j, i: (i, j)),
        out_shape=jax.ShapeDtypeStruct((m, n), BF16),
        scratch_shapes=[pltpu.VMEM((k, tn), BF16)],
        compiler_params=_params(2, _vmem_limit(blocks, [((k, tn), BF16)])),
        name="matmul_bias" if has_bias else "matmul",
    )(*args)


def _ret_decay_tables():
    hh, dk, cs = RET_HEADS, RET_QK_DIM, RET_CHUNK
    log_gamma = jnp.log1p(-(2.0 ** (-5.0 - jnp.arange(hh, dtype=F32))))
    n = jnp.arange(cs, dtype=F32)[:, None]
    q_scale = jnp.exp(log_gamma[None, :] * (n + 1.0 - cs))
    k_scale = jnp.exp(log_gamma[None, :] * (cs - 1.0 - n)) * (dk ** -0.5)
    table = jnp.concatenate([jnp.repeat(q_scale, dk, axis=1), jnp.repeat(k_scale, dk, axis=1)], axis=1)
    chunk_decay = jnp.exp(log_gamma * cs)
    return table, chunk_decay


RET_PROJ_SUB_ROWS = 256


def _ret_in_proj_kernel(x_ref, w_ref, tab_ref, o_ref, wbf_ref, *, n_scaled, n_plain):
    j = pl.program_id(0)
    tm = x_ref.shape[0]
    cs = tab_ref.shape[0]

    @pl.when(pl.program_id(1) == 0)
    def _():
        wbf_ref[...] = w_ref[...].astype(BF16)

    def run(epilogue):
        w = wbf_ref[...]
        for r0 in range(0, tm, RET_PROJ_SUB_ROWS):
            acc = jnp.dot(x_ref[r0:r0 + RET_PROJ_SUB_ROWS, :w.shape[0]], w, preferred_element_type=F32)
            o_ref[r0:r0 + RET_PROJ_SUB_ROWS, :] = epilogue(acc, r0).astype(o_ref.dtype)

    def scaled(acc, r0):
        t0 = r0 % cs
        return acc * tab_ref[t0:t0 + RET_PROJ_SUB_ROWS, :]

    @pl.when(j < n_scaled)
    def _():
        run(scaled)

    @pl.when((j >= n_scaled) & (j < n_scaled + n_plain))
    def _():
        run(lambda acc, r0: acc)

    @pl.when(j >= n_scaled + n_plain)
    def _():
        run(lambda acc, r0: acc / (1.0 + jnp.exp(-acc)))


def _ret_in_proj(x, w, layer, table, *, tm=2048, tn=1024):
    m, pitch = x.shape
    k, n = w.shape[1:]
    cs, scaled_cols = table.shape
    assert scaled_cols % tn == 0 and cs % RET_PROJ_SUB_ROWS == 0 and tm % cs == 0
    n_scaled = scaled_cols // tn
    n_plain = RET_HEADS * RET_V_DIM // tn
    blocks = [((tm, pitch), BF16), ((k, tn), F32), ((cs, tn), F32), ((tm, tn), BF16)]
    return pl.pallas_call(
        functools.partial(_ret_in_proj_kernel, n_scaled=n_scaled, n_plain=n_plain),
        grid=(n // tn, m // tm),
        in_specs=[pl.BlockSpec((tm, pitch), lambda j, i: (i, 0)),
                  pl.BlockSpec((None, k, tn), lambda j, i: (layer, 0, j)),
                  pl.BlockSpec((cs, tn), lambda j, i: (0, jnp.minimum(j, n_scaled - 1)))],
        out_specs=pl.BlockSpec((tm, tn), lambda j, i: (i, j)),
        out_shape=jax.ShapeDtypeStruct((m, n), BF16),
        scratch_shapes=[pltpu.VMEM((k, tn), BF16)],
        compiler_params=_params(2, _vmem_limit(blocks, [((k, tn), BF16)])),
        name="ret_in_proj",
    )(x, w, table)


RET_CHUNKS_PER_STEP = 2


def _side_cast_specs(w, layer, n_steps):
    k, d = w.shape[1:]
    chunk = k // n_steps
    assert k % n_steps == 0 and chunk % (2 * V7X_SUBLANES) == 0
    in_spec = pl.BlockSpec((None, chunk, d), lambda s: (layer, s, 0))
    out_spec = pl.BlockSpec((chunk, d), lambda s: (s, 0))
    return in_spec, out_spec, jax.ShapeDtypeStruct((k, d), BF16), [((chunk, d), F32), ((chunk, d), BF16)]


def _retention_kernel(cd_ref, q_ref, k_ref, v_ref, g_ref, wo_ref, y_ref, wo_bf_ref, state_ref, sbf_ref):
    hh, dk, dv, cs = RET_HEADS, RET_QK_DIM, RET_V_DIM, RET_CHUNK
    wo_bf_ref[...] = wo_ref[...].astype(BF16)

    @pl.when(pl.program_id(0) == 0)
    def _():
        state_ref[...] = jnp.zeros_like(state_ref)
        sbf_ref[...] = jnp.zeros_like(sbf_ref)

    causal = (lax.broadcasted_iota(jnp.int32, (cs, cs), 0)
              >= lax.broadcasted_iota(jnp.int32, (cs, cs), 1))
    for r0 in range(0, q_ref.shape[0], cs):
        rows = slice(r0, r0 + cs)
        for h in range(hh):
            q = q_ref[rows, h * dk:(h + 1) * dk]
            k = k_ref[rows, h * dk:(h + 1) * dk]
            v = v_ref[rows, h * dv:(h + 1) * dv]
            s = lax.dot_general(q, k, (((1,), (1,)), ((), ())), preferred_element_type=F32)
            p = jnp.where(causal, s, 0.0).astype(BF16)
            o = (jnp.dot(p, v, preferred_element_type=F32)
                 + jnp.dot(q, sbf_ref[h], preferred_element_type=F32))
            upd = lax.dot_general(k, v, (((0,), (0,)), ((), ())), preferred_element_type=F32)
            new_state = (state_ref[h] + upd) * cd_ref[h]
            state_ref[h] = new_state
            sbf_ref[h] = new_state.astype(BF16)

            ms = jnp.mean(o * o, axis=-1, keepdims=True)
            gate = g_ref[rows, h * dv:(h + 1) * dv].astype(F32)
            y_ref[rows, h * dv:(h + 1) * dv] = (gate * (o * lax.rsqrt(ms + EPS))).astype(y_ref.dtype)


def _retention(proj, chunk_decay, w_out, layer):
    s = proj.shape[0]
    hh, dk, dv, cs = RET_HEADS, RET_QK_DIM, RET_V_DIM, RET_CHUNK
    qk_w, v_w = hh * dk, hh * dv
    assert 2 * qk_w == v_w
    rows = RET_CHUNKS_PER_STEP * cs
    n_steps = s // rows
    wo_in, wo_out, wo_shape, wo_blocks = _side_cast_specs(w_out, layer, n_steps)
    blocks = [((rows, qk_w), BF16)] * 2 + [((rows, v_w), BF16)] * 3 + wo_blocks
    scratch = [((hh, dk, dv), F32), ((hh, dk, dv), BF16)]
    return pl.pallas_call(
        _retention_kernel,
        grid=(n_steps,),
        in_specs=[pl.BlockSpec(memory_space=pltpu.SMEM),
                  pl.BlockSpec((rows, qk_w), lambda c: (c, 0)),
                  pl.BlockSpec((rows, qk_w), lambda c: (c, 1)),
                  pl.BlockSpec((rows, v_w), lambda c: (c, 1)),
                  pl.BlockSpec((rows, v_w), lambda c: (c, 2)),
                  wo_in],
        out_specs=(pl.BlockSpec((rows, v_w), lambda c: (c, 0)), wo_out),
        out_shape=(jax.ShapeDtypeStruct((s, v_w), BF16), wo_shape),
        scratch_shapes=[pltpu.VMEM((hh, dk, dv), F32), pltpu.VMEM((hh, dk, dv), BF16)],
        compiler_params=_params(1, _vmem_limit(blocks, scratch)),
        name="retention",
    )(chunk_decay, proj, proj, proj, proj, w_out)


def _proj_res_norm_kernel(a_ref, w_ref, res_ref, gain_ref, *out_refs, emit_residual):
    h = res_ref[...] + jnp.dot(a_ref[...], w_ref[...], preferred_element_type=F32)
    ms = jnp.mean(h * h, axis=-1, keepdims=True)
    hn = h * lax.rsqrt(ms + EPS) * gain_ref[...]
    if emit_residual:
        h_ref, hn_ref = out_refs
        h_ref[...] = h
        _store_padded(hn_ref, slice(None), hn)
    else:
        (hn_ref,) = out_refs
        hn_ref[...] = hn.astype(hn_ref.dtype)


def _proj_res_norm(a, w, res, gain, *, emit_residual, tm=512):
    m = a.shape[0]
    k, d = w.shape
    row = lambda s: (s, 0)
    if emit_residual:
        out_shape = (jax.ShapeDtypeStruct((m, d), F32), jax.ShapeDtypeStruct((m, D_MODEL_PITCH), BF16))
        out_specs = (pl.BlockSpec((tm, d), row), pl.BlockSpec((tm, D_MODEL_PITCH), row))
        out_blocks = [((tm, d), F32), ((tm, D_MODEL_PITCH), BF16)]
    else:
        out_shape = jax.ShapeDtypeStruct((m, d), F32)
        out_specs = pl.BlockSpec((tm, d), row)
        out_blocks = [((tm, d), F32)]
    blocks = [((tm, k), BF16), ((tm, d), F32), ((1, d), F32)] + out_blocks
    return pl.pallas_call(
        functools.partial(_proj_res_norm_kernel, emit_residual=emit_residual),
        grid=(m // tm,),
        in_specs=[pl.BlockSpec((tm, k), row),
                  pl.BlockSpec((k, d), lambda s: (0, 0), pipeline_mode=pl.Buffered(1)),
                  pl.BlockSpec((tm, d), row),
                  pl.BlockSpec((1, d), lambda s: (0, 0))],
        out_specs=out_specs,
        out_shape=out_shape,
        compiler_params=_params(1, _vmem_limit(blocks, [((k, d), BF16)])),
        name="proj_res_norm" if emit_residual else "proj_res_final_norm",
    )(a, w, res, gain.reshape(1, d))


FF_FULL_TILES = D_FF // FF_TILE
FF_REM = D_FF % FF_TILE
FF_NUM_TILES = FF_FULL_TILES + 1
FF_EPI_ROWS = 32


FF_SUB_ROWS = 256


def _ffn_up_kernel(x_ref, wa_ref, wb0_ref, wb1_ref, cw_ref, cb_ref, wd_ref, g_ref, wd_bf_ref, wbf_ref,
                   *stage_refs):
    wd_bf_ref[...] = wd_ref[...].astype(BF16)
    j = pl.program_id(0)
    i = pl.program_id(1)
    tm = x_ref.shape[0]
    tf = wa_ref.shape[1]
    head = wb0_ref.shape[1]
    hist = V7X_SUBLANES
    sub_rows = FF_SUB_ROWS
    n_sub = tm // sub_rows
    assert n_sub % 2 == 0
    slots = (stage_refs[0:3], stage_refs[3:6])
    last_u1, last_u2 = slots[1][1], slots[1][2]

    @pl.when((i == 0) & (j == 0))
    def _():
        for _, u1_ref, u2_ref in slots:
            u1_ref[sub_rows:, :] = jnp.zeros((hist, 2 * tf), F32)
            u2_ref[sub_rows:, :] = jnp.zeros((hist, 2 * tf), F32)

    @pl.when(i == 0)
    def _():
        wbf_ref[:, :tf] = wa_ref[...].astype(BF16)
        wbf_ref[:, tf:tf + head] = wb0_ref[...].astype(BF16)
        last_u1[sub_rows:, :] = jnp.zeros((hist, 2 * tf), F32)
        last_u2[sub_rows:, :] = jnp.zeros((hist, 2 * tf), F32)

    @pl.when((i == 0) & (j < FF_FULL_TILES))
    def _():
        wbf_ref[:, tf + head:] = wb1_ref[:, :FF_REM].astype(BF16)

    @pl.when((i == 0) & (j == FF_FULL_TILES))
    def _():
        valid = FF_REM - head
        wbf_ref[:, tf + head:tf + head + valid] = wb1_ref[:, :valid].astype(BF16)
        wbf_ref[:, tf + head + valid:] = jnp.zeros((wbf_ref.shape[0], tf - head - valid), BF16)

    def conv(slot, r0, c0):
        u0_ref, u1_ref, u2_ref = slot
        rows = slice(r0, r0 + FF_EPI_ROWS)
        cols = slice(c0, c0 + V7X_LANES)
        sub = V7X_SUBLANES
        tap = [cw_ref[t * sub:(t + 1) * sub, cols][None] for t in range(CONV_WIDTH)]
        piece = lambda ref: ref[rows, cols].reshape(FF_EPI_ROWS // sub, sub, V7X_LANES)
        c = cb_ref[:, cols][None] + (tap[0] * piece(u2_ref) + tap[1] * piece(u1_ref) + tap[2] * piece(u0_ref))
        return c.reshape(FF_EPI_ROWS, V7X_LANES)

    w = wbf_ref[...]
    for k in range(n_sub):
        s0 = k * sub_rows
        slot = slots[k % 2]
        u0_ref, u1_ref, u2_ref = slot
        _, prev_u1, prev_u2 = slots[(k + 1) % 2]
        u = jnp.dot(x_ref[s0:s0 + sub_rows, :w.shape[0]], w, preferred_element_type=F32)
        u1_ref[0:hist, :] = prev_u1[sub_rows:, :]
        u2_ref[0:hist, :] = prev_u2[sub_rows:, :]
        u0_ref[...] = u
        u1_ref[1:1 + sub_rows, :] = u
        u2_ref[2:2 + sub_rows, :] = u
        for r0 in range(0, sub_rows, FF_EPI_ROWS):
            for c0 in range(0, tf, V7X_LANES):
                a = conv(slot, r0, c0)
                b = conv(slot, r0, tf + c0)
                g_ref[s0 + r0:s0 + r0 + FF_EPI_ROWS, c0:c0 + V7X_LANES] = (
                    a / (1.0 + jnp.exp(-a)) * b).astype(g_ref.dtype)


FF_WD_CHUNK = 128


def _ffn_up(x, w_up, w_down, layer, conv_w, conv_b, *, tm=2048, tf=FF_TILE):
    m, pitch = x.shape
    d = w_up.shape[1]
    nf = FF_NUM_TILES
    head = tf - FF_REM
    assert tf % head == 0 and FF_REM % head == 0
    sub = V7X_SUBLANES
    def per_tile(t):
        taps = t.shape[0]
        t = jnp.pad(t.reshape(taps, 2, D_FF), ((0, 0), (0, 0), (0, D_FF_PAD - D_FF)))
        t = t.reshape(taps, 2, nf, tf).transpose(2, 0, 1, 3).reshape(nf, taps, 1, 2 * tf)
        return jnp.broadcast_to(t, (nf, taps, sub, 2 * tf)).reshape(nf, taps * sub, 2 * tf)
    cw = per_tile(conv_w)
    cb = per_tile(conv_b.reshape(1, -1))
    nm = m // tm
    kd, dd = w_down.shape[1:]
    n_chunks = kd // FF_WD_CHUNK
    assert kd % FF_WD_CHUNK == 0 and n_chunks <= nf * nm
    wd_chunk = lambda j, i: jnp.minimum(j * nm + i, n_chunks - 1)
    blocks = ([((tm, pitch), BF16)] + [((d, tf), F32)] * 2 + [((d, head), F32)]
              + [((CONV_WIDTH * sub, 2 * tf), F32)] + [((sub, 2 * tf), F32)] + [((tm, tf), BF16)]
              + [((FF_WD_CHUNK, dd), F32), ((FF_WD_CHUNK, dd), BF16)])
    slot = [((FF_SUB_ROWS, 2 * tf), F32)] + [((FF_SUB_ROWS + V7X_SUBLANES, 2 * tf), F32)] * 2
    scratch = [((d, 2 * tf), BF16)] + slot * 2
    return pl.pallas_call(
        _ffn_up_kernel,
        grid=(nf, nm),
        in_specs=[pl.BlockSpec((tm, pitch), lambda j, i: (i, 0)),
                  pl.BlockSpec((None, d, tf), lambda j, i: (layer, 0, j)),
                  pl.BlockSpec((None, d, head),
                               lambda j, i: (layer, 0, (FF_FULL_TILES + j) * (tf // head) + FF_REM // head)),
                  pl.BlockSpec((None, d, tf), lambda j, i: (layer, 0, FF_FULL_TILES + 1 + j)),
                  pl.BlockSpec((None, CONV_WIDTH * sub, 2 * tf), lambda j, i: (j, 0, 0)),
                  pl.BlockSpec((None, sub, 2 * tf), lambda j, i: (j, 0, 0)),
                  pl.BlockSpec((None, FF_WD_CHUNK, dd), lambda j, i: (layer, wd_chunk(j, i), 0))],
        out_specs=(pl.BlockSpec((tm, tf), lambda j, i: (i, j)),
                   pl.BlockSpec((FF_WD_CHUNK, dd), lambda j, i: (wd_chunk(j, i), 0))),
        out_shape=(jax.ShapeDtypeStruct((m, D_FF_PAD), BF16), jax.ShapeDtypeStruct((kd, dd), BF16)),
        scratch_shapes=[pltpu.VMEM(*shape_dtype) for shape_dtype in scratch],
        compiler_params=_params(2, _vmem_limit(blocks, scratch)),
        name="ffn_up",
    )(x, w_up, w_up, w_up, cw, cb, w_down)


SWA_BLOCKS_PER_STEP = 1


def _swa_kernel(slopes_ref, sinks_ref, q_ref, kp_ref, kc_ref, vp_ref, vc_ref, wo_ref, o_ref, wo_bf_ref,
                bias_ref):
    wo_bf_ref[...] = wo_ref[...].astype(BF16)
    n = pl.program_id(0)
    blk, dh, grp = ATT_BLOCK, ATT_HEAD_DIM, ATT_GROUP
    lanes = 2 * dh
    pairs = grp // 2

    def fill_bias(table, first_block):
        qi = lax.broadcasted_iota(jnp.int32, (blk, 2 * blk), 0)
        kj = lax.broadcasted_iota(jnp.int32, (blk, 2 * blk), 1)
        dist = blk + qi - kj
        valid = (dist >= 0) & (dist < WINDOW)
        if first_block:
            valid = valid & (kj >= blk)
        dist_f = dist.astype(F32)
        for head in range(ATT_HEADS):
            bias_ref[table, head] = jnp.where(valid, -(slopes_ref[head] * dist_f), -jnp.inf)

    @pl.when(n == 0)
    def _():
        fill_bias(0, True)
        for table in range(1, SWA_BLOCKS_PER_STEP):
            fill_bias(table, False)

    @pl.when(n == 1)
    def _():
        fill_bias(0, False)

    lane_kv = lax.broadcasted_iota(jnp.int32, (2 * blk, lanes), 1)
    lane_o = lax.broadcasted_iota(jnp.int32, (blk, lanes), 1)

    for b in range(SWA_BLOCKS_PER_STEP):
        rows = slice(b * blk, (b + 1) * blk)
        prev_rows = slice((b - 1) * blk, b * blk)
        for pair in range(ATT_KV_HEADS // 2):
            cols = slice(pair * lanes, (pair + 1) * lanes)
            k_prev = kp_ref[:, cols] if b == 0 else kc_ref[prev_rows, cols]
            v_prev = vp_ref[:, cols] if b == 0 else vc_ref[prev_rows, cols]
            k2 = jnp.concatenate([k_prev, kc_ref[rows, cols]], axis=0).astype(F32) * (dh ** -0.5)
            v2 = jnp.concatenate([v_prev, vc_ref[rows, cols]], axis=0).astype(F32)
            k2s = pltpu.roll(k2, dh, axis=1)
            v2s = pltpu.roll(v2, dh, axis=1)
            for sub in range(2):
                kvh = 2 * pair + sub
                k_lo, k_hi = (k2, k2s) if sub == 0 else (k2s, k2)
                v_lo, v_hi = (v2, v2s) if sub == 0 else (v2s, v2)
                kz = (jnp.where(lane_kv < dh, k_lo, 0.0).astype(BF16),
                      jnp.where(lane_kv >= dh, k_hi, 0.0).astype(BF16))
                vz = (jnp.where(lane_kv < dh, v_lo, 0.0).astype(BF16),
                      jnp.where(lane_kv >= dh, v_hi, 0.0).astype(BF16))
                qcol0 = kvh * grp * dh
                q_stack = jnp.concatenate(
                    [q_ref[rows, qcol0 + hp * lanes:qcol0 + (hp + 1) * lanes] for hp in range(pairs)], axis=0)
                acc = None
                inv = [[None, None] for _ in range(pairs)]
                for par in range(2):
                    s_all = lax.dot_general(q_stack, kz[par], (((1,), (1,)), ((), ())),
                                            preferred_element_type=F32)
                    e_parts = []
                    for hp in range(pairs):
                        head = kvh * grp + 2 * hp + par
                        s = s_all[hp * blk:(hp + 1) * blk] + bias_ref[b, head]
                        sink = sinks_ref[head]
                        m = jnp.maximum(jnp.max(s, axis=-1, keepdims=True), sink)
                        e = jnp.exp(s - m)
                        denom = jnp.sum(e, axis=-1, keepdims=True) + jnp.exp(sink - m)
                        inv[hp][par] = 1.0 / denom
                        e_parts.append(e.astype(BF16))
                    pv = jnp.dot(jnp.concatenate(e_parts, axis=0), vz[par], preferred_element_type=F32)
                    acc = pv if acc is None else acc + pv
                for hp in range(pairs):
                    scale = jnp.where(lane_o < dh, inv[hp][0], inv[hp][1])
                    o_ref[rows, qcol0 + hp * lanes:qcol0 + (hp + 1) * lanes] = (
                        acc[hp * blk:(hp + 1) * blk] * scale).astype(o_ref.dtype)


def _swa(qkv, sinks, w_out, layer):
    s = qkv.shape[0]
    blk = ATT_BLOCK
    dq = ATT_HEADS * ATT_HEAD_DIM
    dkv = ATT_KV_HEADS * ATT_HEAD_DIM
    k_blk = dq // dkv
    v_blk = k_blk + 1
    slopes = 2.0 ** (-8.0 * jnp.arange(1, ATT_HEADS + 1, dtype=F32) / ATT_HEADS)
    per_step = SWA_BLOCKS_PER_STEP
    rows = per_step * blk
    prev = lambda n: jnp.maximum(n * per_step - 1, 0)
    n_steps = s // rows
    wo_in, wo_out, wo_shape, wo_blocks = _side_cast_specs(w_out, layer, n_steps)
    blocks = ([((rows, dq), BF16)] * 2 + [((blk, dkv), BF16)] * 2 + [((rows, dkv), BF16)] * 2
              + wo_blocks)
    bias = ((per_step, ATT_HEADS, blk, 2 * blk), F32)
    return pl.pallas_call(
        _swa_kernel,
        grid=(n_steps,),
        in_specs=[pl.BlockSpec(memory_space=pltpu.SMEM),
                  pl.BlockSpec(memory_space=pltpu.SMEM),
                  pl.BlockSpec((rows, dq), lambda n: (n, 0)),
                  pl.BlockSpec((blk, dkv), lambda n: (prev(n), k_blk)),
                  pl.BlockSpec((rows, dkv), lambda n: (n, k_blk)),
                  pl.BlockSpec((blk, dkv), lambda n: (prev(n), v_blk)),
                  pl.BlockSpec((rows, dkv), lambda n: (n, v_blk)),
                  wo_in],
        out_specs=(pl.BlockSpec((rows, dq), lambda n: (n, 0)), wo_out),
        out_shape=(jax.ShapeDtypeStruct((s, dq), BF16), wo_shape),
        scratch_shapes=[pltpu.VMEM(*bias)],
        compiler_params=_params(1, _vmem_limit(blocks, [bias])),
        name="swa",
    )(slopes, sinks.astype(F32), qkv, qkv, qkv, qkv, qkv, w_out)


def kernel(x, norm_mix_g, ret_w_in, ret_w_out, att_w_qkv, att_b_qkv, att_sinks, att_w_out,
           norm_ffn_g, ffn_w_up, ffn_conv_w, ffn_conv_b, ffn_w_down, final_norm_g):
    batch, seq, d = x.shape
    assert batch == 1 and d == D_MODEL
    depth = norm_mix_g.shape[0]

    h = x.reshape(seq, d)
    hn = _rmsnorm(h, norm_mix_g[0])
    for i in range(depth):
        j = i // N_MIXERS
        if i % N_MIXERS == 0:
            decay_table, chunk_decay = _ret_decay_tables()
            proj = _ret_in_proj(hn, ret_w_in, j, decay_table)
            mixed, w_out = _retention(proj, chunk_decay, ret_w_out, j)
        else:
            qkv = _matmul(hn, att_w_qkv, j, att_b_qkv, tm=1024, tn=1280)
            mixed, w_out = _swa(qkv, att_sinks[j], att_w_out, j)
        h, hn = _proj_res_norm(mixed, w_out, h, norm_ffn_g[i], emit_residual=True)

        gated, w_down = _ffn_up(hn, ffn_w_up, ffn_w_down, i, ffn_conv_w[i], ffn_conv_b[i])
        if i + 1 < depth:
            h, hn = _proj_res_norm(gated, w_down, h, norm_mix_g[i + 1], emit_residual=True)
        else:
            out = _proj_res_norm(gated, w_down, h, final_norm_g, emit_residual=False)
    return out.reshape(batch, seq, d)
```

```python
import functools

import jax
import jax.numpy as jnp
from jax import lax
from jax.experimental import pallas as pl
from jax.experimental.pallas import tpu as pltpu

F32 = jnp.float32
BF16 = jnp.bfloat16

D_MODEL = 2048
N_MIXERS = 2

RET_HEADS = 8
RET_QK_DIM = D_MODEL // RET_HEADS
RET_V_DIM = 2 * D_MODEL // RET_HEADS
RET_CHUNK = 256

ATT_HEAD_DIM = 64
ATT_HEADS = D_MODEL // ATT_HEAD_DIM
ATT_KV_HEADS = ATT_HEADS // 8
ATT_GROUP = ATT_HEADS // ATT_KV_HEADS
WINDOW = 128
ATT_BLOCK = 128

D_FF = ((8 * D_MODEL // 3 + 127) // 128) * 128
CONV_WIDTH = 3
EPS = 1e-6

V7X_LANES = 128
V7X_SUBLANES = 8
V7X_VMEM_BYTES = 64 * 1024 * 1024
V7X_COMPILER_SCRATCH_BYTES = 8 * 1024 * 1024
V7X_VMEM_MIN_RESERVE_BYTES = 52 * 1024 * 1024

FF_TILE = 512
D_FF_PAD = ((D_FF + FF_TILE - 1) // FF_TILE) * FF_TILE

D_MODEL_PITCH = D_MODEL + V7X_LANES


def _nbytes(shape, dtype):
    n = 1
    for s in shape:
        n *= s
    return n * jnp.dtype(dtype).itemsize


def _vmem_limit(pipelined, resident=()):
    total = 2 * sum(_nbytes(s, d) for s, d in pipelined)
    total += sum(_nbytes(s, d) for s, d in resident)
    total += V7X_COMPILER_SCRATCH_BYTES
    assert total <= V7X_VMEM_BYTES, total
    return max(total, V7X_VMEM_MIN_RESERVE_BYTES)


def _params(n_grid, vmem_limit):
    return pltpu.CompilerParams(
        dimension_semantics=("arbitrary",) * n_grid, vmem_limit_bytes=vmem_limit)


def _store_padded(ref, rows, value):
    n, d = value.shape
    ref[rows, :d] = value.astype(ref.dtype)
    ref[rows, d:] = jnp.zeros((n, ref.shape[1] - d), ref.dtype)


def _rmsnorm_kernel(x_ref, g_ref, o_ref):
    x = x_ref[...]
    ms = jnp.mean(x * x, axis=-1, keepdims=True)
    _store_padded(o_ref, slice(None), x * lax.rsqrt(ms + EPS) * g_ref[...])


def _rmsnorm(x, gain, *, tm=1024):
    m, d = x.shape
    blocks = [((tm, d), F32), ((1, d), F32), ((tm, D_MODEL_PITCH), BF16)]
    return pl.pallas_call(
        _rmsnorm_kernel,
        grid=(m // tm,),
        in_specs=[pl.BlockSpec((tm, d), lambda i: (i, 0)),
                  pl.BlockSpec((1, d), lambda i: (0, 0))],
        out_specs=pl.BlockSpec((tm, D_MODEL_PITCH), lambda i: (i, 0)),
        out_shape=jax.ShapeDtypeStruct((m, D_MODEL_PITCH), BF16),
        compiler_params=_params(1, _vmem_limit(blocks)),
        name="rmsnorm",
    )(x, gain.reshape(1, d))


def _matmul_bias_kernel(x_ref, w_ref, b_ref, side_ref, o_ref, side_bf_ref, wbf_ref):
    side_bf_ref[...] = side_ref[...].astype(BF16)

    @pl.when(pl.program_id(1) == 0)
    def _():
        wbf_ref[...] = w_ref[...].astype(BF16)

    k = wbf_ref.shape[0]
    acc = jnp.dot(x_ref[:, :k], wbf_ref[...], preferred_element_type=F32) + b_ref[...]
    o_ref[...] = acc.astype(o_ref.dtype)


def _matmul_bias(x, w, bias, layer, side_w, *, tm, tn):
    m, pitch = x.shape
    k, n = w.shape[1:]
    grid = (n // tn, m // tm)
    side_in, side_out, side_shape, side_blocks = _side_cast_specs(side_w, layer, grid)
    blocks = [((tm, pitch), BF16), ((k, tn), F32), ((1, tn), F32), ((tm, tn), BF16)] + side_blocks
    return pl.pallas_call(
        _matmul_bias_kernel,
        grid=grid,
        in_specs=[pl.BlockSpec((tm, pitch), lambda j, i: (i, 0)),
                  pl.BlockSpec((None, k, tn), lambda j, i: (layer, 0, j)),
                  pl.BlockSpec((None, 1, tn), lambda j, i: (layer, 0, j)),
                  side_in],
        out_specs=(pl.BlockSpec((tm, tn), lambda j, i: (i, j)), side_out),
        out_shape=(jax.ShapeDtypeStruct((m, n), BF16), side_shape),
        scratch_shapes=[pltpu.VMEM((k, tn), BF16)],
        compiler_params=_params(2, _vmem_limit(blocks, [((k, tn), BF16)])),
        name="matmul_bias",
    )(x, w, bias.reshape(bias.shape[0], 1, n), side_w)


def _ret_decay_tables():
    hh, dk, cs = RET_HEADS, RET_QK_DIM, RET_CHUNK
    log_gamma = jnp.log1p(-(2.0 ** (-5.0 - jnp.arange(hh, dtype=F32))))
    n = jnp.arange(cs, dtype=F32)[:, None]
    q_scale = jnp.exp(log_gamma[None, :] * (n + 1.0 - cs))
    k_scale = jnp.exp(log_gamma[None, :] * (cs - 1.0 - n)) * (dk ** -0.5)
    table = jnp.concatenate([jnp.repeat(q_scale, dk, axis=1), jnp.repeat(k_scale, dk, axis=1)], axis=1)
    chunk_decay = jnp.exp(log_gamma * cs)
    return table, chunk_decay


RET_PROJ_SUB_ROWS = 512


def _ret_in_proj_kernel(x_ref, w_ref, tab_ref, side_ref, o_ref, side_bf_ref, wbf_ref, *, n_scaled, n_plain):
    side_bf_ref[...] = side_ref[...].astype(BF16)
    j = pl.program_id(0)
    tm = x_ref.shape[0]
    cs = tab_ref.shape[0]

    @pl.when(pl.program_id(1) == 0)
    def _():
        wbf_ref[...] = w_ref[...].astype(BF16)

    def run(epilogue):
        w = wbf_ref[...]
        for r0 in range(0, tm, RET_PROJ_SUB_ROWS):
            acc = jnp.dot(x_ref[r0:r0 + RET_PROJ_SUB_ROWS, :w.shape[0]], w, preferred_element_type=F32)
            o_ref[r0:r0 + RET_PROJ_SUB_ROWS, :] = epilogue(acc, r0).astype(o_ref.dtype)

    def scaled(acc, r0):
        del r0
        rows, cols = acc.shape
        return (acc.reshape(rows // cs, cs, cols) * tab_ref[...][None]).reshape(rows, cols)

    @pl.when(j < n_scaled)
    def _():
        run(scaled)

    @pl.when((j >= n_scaled) & (j < n_scaled + n_plain))
    def _():
        run(lambda acc, r0: acc)

    @pl.when(j >= n_scaled + n_plain)
    def _():
        run(lambda acc, r0: acc / (1.0 + jnp.exp(-acc)))


def _ret_in_proj(x, w, layer, table, side_w, *, tm=2048, tn=1024):
    m, pitch = x.shape
    k, n = w.shape[1:]
    cs, scaled_cols = table.shape
    assert scaled_cols % tn == 0 and RET_PROJ_SUB_ROWS % cs == 0 and tm % RET_PROJ_SUB_ROWS == 0
    n_scaled = scaled_cols // tn
    n_plain = RET_HEADS * RET_V_DIM // tn
    grid = (n // tn, m // tm)
    side_in, side_out, side_shape, side_blocks = _side_cast_specs(side_w, layer, grid)
    blocks = [((tm, pitch), BF16), ((k, tn), F32), ((cs, tn), F32), ((tm, tn), BF16)] + side_blocks
    return pl.pallas_call(
        functools.partial(_ret_in_proj_kernel, n_scaled=n_scaled, n_plain=n_plain),
        grid=grid,
        in_specs=[pl.BlockSpec((tm, pitch), lambda j, i: (i, 0)),
                  pl.BlockSpec((None, k, tn), lambda j, i: (layer, 0, j)),
                  pl.BlockSpec((cs, tn), lambda j, i: (0, jnp.minimum(j, n_scaled - 1))),
                  side_in],
        out_specs=(pl.BlockSpec((tm, tn), lambda j, i: (i, j)), side_out),
        out_shape=(jax.ShapeDtypeStruct((m, n), BF16), side_shape),
        scratch_shapes=[pltpu.VMEM((k, tn), BF16)],
        compiler_params=_params(2, _vmem_limit(blocks, [((k, tn), BF16)])),
        name="ret_in_proj",
    )(x, w, table, side_w)


RET_CHUNKS_PER_STEP = 2


SIDE_CAST_ROWS = 128


def _side_cast_specs(w, layer, grid):
    k, d = w.shape[1:]
    n_chunks = k // SIDE_CAST_ROWS
    n_steps = 1
    for g in grid:
        n_steps *= g
    assert k % SIDE_CAST_ROWS == 0 and n_chunks <= n_steps

    def chunk_of(*idx):
        step = 0
        for g, i in zip(grid, idx):
            step = step * g + i
        return jnp.minimum(step, n_chunks - 1)

    in_spec = pl.BlockSpec((None, SIDE_CAST_ROWS, d), lambda *idx: (layer, chunk_of(*idx), 0))
    out_spec = pl.BlockSpec((SIDE_CAST_ROWS, d), lambda *idx: (chunk_of(*idx), 0))
    blocks = [((SIDE_CAST_ROWS, d), F32), ((SIDE_CAST_ROWS, d), BF16)]
    return in_spec, out_spec, jax.ShapeDtypeStruct((k, d), BF16), blocks


def _retention_kernel(cd_ref, q_ref, k_ref, v_ref, g_ref, y_ref, state_ref, sbf_ref):
    hh, dk, dv, cs = RET_HEADS, RET_QK_DIM, RET_V_DIM, RET_CHUNK

    @pl.when(pl.program_id(0) == 0)
    def _():
        state_ref[...] = jnp.zeros_like(state_ref)
        sbf_ref[...] = jnp.zeros_like(sbf_ref)

    causal = (lax.broadcasted_iota(jnp.int32, (cs, cs), 0)
              >= lax.broadcasted_iota(jnp.int32, (cs, cs), 1))
    for r0 in range(0, q_ref.shape[0], cs):
        rows = slice(r0, r0 + cs)
        for h in range(hh):
            q = q_ref[rows, h * dk:(h + 1) * dk]
            k = k_ref[rows, h * dk:(h + 1) * dk]
            v = v_ref[rows, h * dv:(h + 1) * dv]
            s = lax.dot_general(q, k, (((1,), (1,)), ((), ())), preferred_element_type=F32)
            p = jnp.where(causal, s, 0.0).astype(BF16)
            o = (jnp.dot(p, v, preferred_element_type=F32)
                 + jnp.dot(q, sbf_ref[h], preferred_element_type=F32))
            upd = lax.dot_general(k, v, (((0,), (0,)), ((), ())), preferred_element_type=F32)
            new_state = (state_ref[h] + upd) * cd_ref[h]
            state_ref[h] = new_state
            sbf_ref[h] = new_state.astype(BF16)

            ms = jnp.mean(o * o, axis=-1, keepdims=True)
            gate = g_ref[rows, h * dv:(h + 1) * dv].astype(F32)
            y_ref[rows, h * dv:(h + 1) * dv] = (gate * (o * lax.rsqrt(ms + EPS))).astype(y_ref.dtype)


def _retention(proj, chunk_decay):
    s = proj.shape[0]
    hh, dk, dv, cs = RET_HEADS, RET_QK_DIM, RET_V_DIM, RET_CHUNK
    qk_w, v_w = hh * dk, hh * dv
    assert 2 * qk_w == v_w
    rows = RET_CHUNKS_PER_STEP * cs
    blocks = [((rows, qk_w), BF16)] * 2 + [((rows, v_w), BF16)] * 3
    scratch = [((hh, dk, dv), F32), ((hh, dk, dv), BF16)]
    return pl.pallas_call(
        _retention_kernel,
        grid=(s // rows,),
        in_specs=[pl.BlockSpec(memory_space=pltpu.SMEM),
                  pl.BlockSpec((rows, qk_w), lambda c: (c, 0)),
                  pl.BlockSpec((rows, qk_w), lambda c: (c, 1)),
                  pl.BlockSpec((rows, v_w), lambda c: (c, 1)),
                  pl.BlockSpec((rows, v_w), lambda c: (c, 2))],
        out_specs=pl.BlockSpec((rows, v_w), lambda c: (c, 0)),
        out_shape=jax.ShapeDtypeStruct((s, v_w), BF16),
        scratch_shapes=[pltpu.VMEM((hh, dk, dv), F32), pltpu.VMEM((hh, dk, dv), BF16)],
        compiler_params=_params(1, _vmem_limit(blocks, scratch)),
        name="retention",
    )(chunk_decay, proj, proj, proj, proj)


def _proj_res_norm_kernel(a_ref, w_ref, res_ref, gain_ref, *out_refs, emit_residual):
    h = res_ref[...] + jnp.dot(a_ref[...], w_ref[...], preferred_element_type=F32)
    ms = jnp.mean(h * h, axis=-1, keepdims=True)
    hn = h * lax.rsqrt(ms + EPS) * gain_ref[...]
    if emit_residual:
        h_ref, hn_ref = out_refs
        h_ref[...] = h
        _store_padded(hn_ref, slice(None), hn)
    else:
        (hn_ref,) = out_refs
        hn_ref[...] = hn.astype(hn_ref.dtype)


def _proj_res_norm(a, w, res, gain, *, emit_residual, tm=512):
    m = a.shape[0]
    k, d = w.shape
    row = lambda s: (s, 0)
    if emit_residual:
        out_shape = (jax.ShapeDtypeStruct((m, d), F32), jax.ShapeDtypeStruct((m, D_MODEL_PITCH), BF16))
        out_specs = (pl.BlockSpec((tm, d), row), pl.BlockSpec((tm, D_MODEL_PITCH), row))
        out_blocks = [((tm, d), F32), ((tm, D_MODEL_PITCH), BF16)]
    else:
        out_shape = jax.ShapeDtypeStruct((m, d), F32)
        out_specs = pl.BlockSpec((tm, d), row)
        out_blocks = [((tm, d), F32)]
    blocks = [((tm, k), BF16), ((tm, d), F32), ((1, d), F32)] + out_blocks
    return pl.pallas_call(
        functools.partial(_proj_res_norm_kernel, emit_residual=emit_residual),
        grid=(m // tm,),
        in_specs=[pl.BlockSpec((tm, k), row),
                  pl.BlockSpec((k, d), lambda s: (0, 0), pipeline_mode=pl.Buffered(1)),
                  pl.BlockSpec((tm, d), row),
                  pl.BlockSpec((1, d), lambda s: (0, 0))],
        out_specs=out_specs,
        out_shape=out_shape,
        compiler_params=_params(1, _vmem_limit(blocks, [((k, d), BF16)])),
        name="proj_res_norm" if emit_residual else "proj_res_final_norm",
    )(a, w, res, gain.reshape(1, d))


FF_FULL_TILES = D_FF // FF_TILE
FF_REM = D_FF % FF_TILE
FF_NUM_TILES = FF_FULL_TILES + 1
FF_EPI_ROWS = 32


FF_SUB_ROWS = 256


def _ffn_up_kernel(x_ref, wa_ref, wb0_ref, wb1_ref, cw_ref, cb_ref, wd_ref, g_ref, wd_bf_ref, wbf_ref,
                   *stage_refs):
    wd_bf_ref[...] = wd_ref[...].astype(BF16)
    j = pl.program_id(0)
    i = pl.program_id(1)
    tm = x_ref.shape[0]
    tf = wa_ref.shape[1]
    head = wb0_ref.shape[1]
    hist = V7X_SUBLANES
    sub_rows = FF_SUB_ROWS
    n_sub = tm // sub_rows
    assert n_sub % 2 == 0
    slots = (stage_refs[0:3], stage_refs[3:6])
    last_u1, last_u2 = slots[1][1], slots[1][2]

    @pl.when((i == 0) & (j == 0))
    def _():
        for _, u1_ref, u2_ref in slots:
            u1_ref[sub_rows:, :] = jnp.zeros((hist, 2 * tf), F32)
            u2_ref[sub_rows:, :] = jnp.zeros((hist, 2 * tf), F32)

    @pl.when(i == 0)
    def _():
        wbf_ref[:, :tf] = wa_ref[...].astype(BF16)
        wbf_ref[:, tf:tf + head] = wb0_ref[...].astype(BF16)
        last_u1[sub_rows:, :] = jnp.zeros((hist, 2 * tf), F32)
        last_u2[sub_rows:, :] = jnp.zeros((hist, 2 * tf), F32)

    @pl.when((i == 0) & (j < FF_FULL_TILES))
    def _():
        wbf_ref[:, tf + head:] = wb1_ref[:, :FF_REM].astype(BF16)

    @pl.when((i == 0) & (j == FF_FULL_TILES))
    def _():
        valid = FF_REM - head
        wbf_ref[:, tf + head:tf + head + valid] = wb1_ref[:, :valid].astype(BF16)
        wbf_ref[:, tf + head + valid:] = jnp.zeros((wbf_ref.shape[0], tf - head - valid), BF16)

    def conv(slot, r0, c0):
        u0_ref, u1_ref, u2_ref = slot
        rows = slice(r0, r0 + FF_EPI_ROWS)
        cols = slice(c0, c0 + V7X_LANES)
        sub = V7X_SUBLANES
        tap = [cw_ref[t * sub:(t + 1) * sub, cols][None] for t in range(CONV_WIDTH)]
        piece = lambda ref: ref[rows, cols].reshape(FF_EPI_ROWS // sub, sub, V7X_LANES)
        c = cb_ref[:, cols][None] + (tap[0] * piece(u2_ref) + tap[1] * piece(u1_ref) + tap[2] * piece(u0_ref))
        return c.reshape(FF_EPI_ROWS, V7X_LANES)

    w = wbf_ref[...]
    for k in range(n_sub):
        s0 = k * sub_rows
        slot = slots[k % 2]
        u0_ref, u1_ref, u2_ref = slot
        _, prev_u1, prev_u2 = slots[(k + 1) % 2]
        u = jnp.dot(x_ref[s0:s0 + sub_rows, :w.shape[0]], w, preferred_element_type=F32)
        u1_ref[0:hist, :] = prev_u1[sub_rows:, :]
        u2_ref[0:hist, :] = prev_u2[sub_rows:, :]
        u0_ref[...] = u
        u1_ref[1:1 + sub_rows, :] = u
        u2_ref[2:2 + sub_rows, :] = u
        for r0 in range(0, sub_rows, FF_EPI_ROWS):
            for c0 in range(0, tf, V7X_LANES):
                a = conv(slot, r0, c0)
                b = conv(slot, r0, tf + c0)
                g_ref[s0 + r0:s0 + r0 + FF_EPI_ROWS, c0:c0 + V7X_LANES] = (
                    a / (1.0 + jnp.exp(-a)) * b).astype(g_ref.dtype)


def _ffn_up(x, w_up, w_down, layer, conv_w, conv_b, *, tm=2048, tf=FF_TILE):
    m, pitch = x.shape
    d = w_up.shape[1]
    nf = FF_NUM_TILES
    head = tf - FF_REM
    assert tf % head == 0 and FF_REM % head == 0
    sub = V7X_SUBLANES
    def per_tile(t):
        taps = t.shape[0]
        t = jnp.pad(t.reshape(taps, 2, D_FF), ((0, 0), (0, 0), (0, D_FF_PAD - D_FF)))
        t = t.reshape(taps, 2, nf, tf).transpose(2, 0, 1, 3).reshape(nf, taps, 1, 2 * tf)
        return jnp.broadcast_to(t, (nf, taps, sub, 2 * tf)).reshape(nf, taps * sub, 2 * tf)
    cw = per_tile(conv_w)
    cb = per_tile(conv_b.reshape(1, -1))
    grid = (nf, m // tm)
    wd_in, wd_out, wd_shape, wd_blocks = _side_cast_specs(w_down, layer, grid)
    blocks = ([((tm, pitch), BF16)] + [((d, tf), F32)] * 2 + [((d, head), F32)]
              + [((CONV_WIDTH * sub, 2 * tf), F32)] + [((sub, 2 * tf), F32)] + [((tm, tf), BF16)]
              + wd_blocks)
    slot = [((FF_SUB_ROWS, 2 * tf), F32)] + [((FF_SUB_ROWS + V7X_SUBLANES, 2 * tf), F32)] * 2
    scratch = [((d, 2 * tf), BF16)] + slot * 2
    return pl.pallas_call(
        _ffn_up_kernel,
        grid=grid,
        in_specs=[pl.BlockSpec((tm, pitch), lambda j, i: (i, 0)),
                  pl.BlockSpec((None, d, tf), lambda j, i: (layer, 0, j)),
                  pl.BlockSpec((None, d, head),
                               lambda j, i: (layer, 0, (FF_FULL_TILES + j) * (tf // head) + FF_REM // head)),
                  pl.BlockSpec((None, d, tf), lambda j, i: (layer, 0, FF_FULL_TILES + 1 + j)),
                  pl.BlockSpec((None, CONV_WIDTH * sub, 2 * tf), lambda j, i: (j, 0, 0)),
                  pl.BlockSpec((None, sub, 2 * tf), lambda j, i: (j, 0, 0)),
                  wd_in],
        out_specs=(pl.BlockSpec((tm, tf), lambda j, i: (i, j)), wd_out),
        out_shape=(jax.ShapeDtypeStruct((m, D_FF_PAD), BF16), wd_shape),
        scratch_shapes=[pltpu.VMEM(*shape_dtype) for shape_dtype in scratch],
        compiler_params=_params(2, _vmem_limit(blocks, scratch)),
        name="ffn_up",
    )(x, w_up, w_up, w_up, cw, cb, w_down)


SWA_BLOCKS_PER_STEP = 1


def _swa_kernel(slopes_ref, sinks_ref, q_ref, kp_ref, kc_ref, vp_ref, vc_ref, o_ref, bias_ref):
    n = pl.program_id(0)
    blk, dh, grp = ATT_BLOCK, ATT_HEAD_DIM, ATT_GROUP
    lanes = 2 * dh
    pairs = grp // 2

    def fill_bias(table, first_block):
        qi = lax.broadcasted_iota(jnp.int32, (blk, 2 * blk), 0)
        kj = lax.broadcasted_iota(jnp.int32, (blk, 2 * blk), 1)
        dist = blk + qi - kj
        valid = (dist >= 0) & (dist < WINDOW)
        if first_block:
            valid = valid & (kj >= blk)
        dist_f = dist.astype(F32)
        for head in range(ATT_HEADS):
            bias_ref[table, head] = jnp.where(valid, -(slopes_ref[head] * dist_f), -jnp.inf)

    @pl.when(n == 0)
    def _():
        fill_bias(0, True)
        for table in range(1, SWA_BLOCKS_PER_STEP):
            fill_bias(table, False)

    @pl.when(n == 1)
    def _():
        fill_bias(0, False)

    lane_kv = lax.broadcasted_iota(jnp.int32, (2 * blk, lanes), 1)
    lane_o = lax.broadcasted_iota(jnp.int32, (blk, lanes), 1)

    for b in range(SWA_BLOCKS_PER_STEP):
        rows = slice(b * blk, (b + 1) * blk)
        prev_rows = slice((b - 1) * blk, b * blk)
        for pair in range(ATT_KV_HEADS // 2):
            cols = slice(pair * lanes, (pair + 1) * lanes)
            k_prev = kp_ref[:, cols] if b == 0 else kc_ref[prev_rows, cols]
            v_prev = vp_ref[:, cols] if b == 0 else vc_ref[prev_rows, cols]
            k2 = jnp.concatenate([k_prev, kc_ref[rows, cols]], axis=0).astype(F32) * (dh ** -0.5)
            v2 = jnp.concatenate([v_prev, vc_ref[rows, cols]], axis=0).astype(F32)
            k2s = pltpu.roll(k2, dh, axis=1)
            v2s = pltpu.roll(v2, dh, axis=1)
            for sub in range(2):
                kvh = 2 * pair + sub
                k_lo, k_hi = (k2, k2s) if sub == 0 else (k2s, k2)
                v_lo, v_hi = (v2, v2s) if sub == 0 else (v2s, v2)
                kz = (jnp.where(lane_kv < dh, k_lo, 0.0).astype(BF16),
                      jnp.where(lane_kv >= dh, k_hi, 0.0).astype(BF16))
                vz = (jnp.where(lane_kv < dh, v_lo, 0.0).astype(BF16),
                      jnp.where(lane_kv >= dh, v_hi, 0.0).astype(BF16))
                qcol0 = kvh * grp * dh
                q_stack = jnp.concatenate(
                    [q_ref[rows, qcol0 + hp * lanes:qcol0 + (hp + 1) * lanes] for hp in range(pairs)], axis=0)
                acc = None
                inv = [[None, None] for _ in range(pairs)]
                for par in range(2):
                    s_all = lax.dot_general(q_stack, kz[par], (((1,), (1,)), ((), ())),
                                            preferred_element_type=F32)
                    e_parts = []
                    for hp in range(pairs):
                        head = kvh * grp + 2 * hp + par
                        s = s_all[hp * blk:(hp + 1) * blk] + bias_ref[b, head]
                        sink = sinks_ref[head]
                        m = jnp.maximum(jnp.max(s, axis=-1, keepdims=True), sink)
                        e = jnp.exp(s - m)
                        denom = jnp.sum(e, axis=-1, keepdims=True) + jnp.exp(sink - m)
                        inv[hp][par] = 1.0 / denom
                        e_parts.append(e.astype(BF16))
                    pv = jnp.dot(jnp.concatenate(e_parts, axis=0), vz[par], preferred_element_type=F32)
                    acc = pv if acc is None else acc + pv
                for hp in range(pairs):
                    scale = jnp.where(lane_o < dh, inv[hp][0], inv[hp][1])
                    o_ref[rows, qcol0 + hp * lanes:qcol0 + (hp + 1) * lanes] = (
                        acc[hp * blk:(hp + 1) * blk] * scale).astype(o_ref.dtype)


def _swa(qkv, sinks):
    s = qkv.shape[0]
    blk = ATT_BLOCK
    dq = ATT_HEADS * ATT_HEAD_DIM
    dkv = ATT_KV_HEADS * ATT_HEAD_DIM
    k_blk = dq // dkv
    v_blk = k_blk + 1
    slopes = 2.0 ** (-8.0 * jnp.arange(1, ATT_HEADS + 1, dtype=F32) / ATT_HEADS)
    per_step = SWA_BLOCKS_PER_STEP
    rows = per_step * blk
    prev = lambda n: jnp.maximum(n * per_step - 1, 0)
    blocks = [((rows, dq), BF16)] * 2 + [((blk, dkv), BF16)] * 2 + [((rows, dkv), BF16)] * 2
    bias = ((per_step, ATT_HEADS, blk, 2 * blk), F32)
    return pl.pallas_call(
        _swa_kernel,
        grid=(s // rows,),
        in_specs=[pl.BlockSpec(memory_space=pltpu.SMEM),
                  pl.BlockSpec(memory_space=pltpu.SMEM),
                  pl.BlockSpec((rows, dq), lambda n: (n, 0)),
                  pl.BlockSpec((blk, dkv), lambda n: (prev(n), k_blk)),
                  pl.BlockSpec((rows, dkv), lambda n: (n, k_blk)),
                  pl.BlockSpec((blk, dkv), lambda n: (prev(n), v_blk)),
                  pl.BlockSpec((rows, dkv), lambda n: (n, v_blk))],
        out_specs=pl.BlockSpec((rows, dq), lambda n: (n, 0)),
        out_shape=jax.ShapeDtypeStruct((s, dq), BF16),
        scratch_shapes=[pltpu.VMEM(*bias)],
        compiler_params=_params(1, _vmem_limit(blocks, [bias])),
        name="swa",
    )(slopes, sinks.astype(F32), qkv, qkv, qkv, qkv, qkv)


def kernel(x, norm_mix_g, ret_w_in, ret_w_out, att_w_qkv, att_b_qkv, att_sinks, att_w_out,
           norm_ffn_g, ffn_w_up, ffn_conv_w, ffn_conv_b, ffn_w_down, final_norm_g):
    batch, seq, d = x.shape
    assert batch == 1 and d == D_MODEL
    depth = norm_mix_g.shape[0]

    h = x.reshape(seq, d)
    hn = _rmsnorm(h, norm_mix_g[0])
    for i in range(depth):
        j = i // N_MIXERS
        if i % N_MIXERS == 0:
            decay_table, chunk_decay = _ret_decay_tables()
            proj, w_out = _ret_in_proj(hn, ret_w_in, j, decay_table, ret_w_out)
            mixed = _retention(proj, chunk_decay)
        else:
            qkv, w_out = _matmul_bias(hn, att_w_qkv, att_b_qkv, j, att_w_out, tm=1024, tn=1280)
            mixed = _swa(qkv, att_sinks[j])
        h, hn = _proj_res_norm(mixed, w_out, h, norm_ffn_g[i], emit_residual=True)

        gated, w_down = _ffn_up(hn, ffn_w_up, ffn_w_down, i, ffn_conv_w[i], ffn_conv_b[i])
        if i + 1 < depth:
            h, hn = _proj_res_norm(gated, w_down, h, norm_mix_g[i + 1], emit_residual=True)
        else:
            out = _proj_res_norm(gated, w_down, h, final_norm_g, emit_residual=False)
    return out.reshape(batch, seq, d)
```

```python
import functools

import jax
import jax.numpy as jnp
from jax import lax
from jax.experimental import pallas as pl
from jax.experimental.pallas import tpu as pltpu

F32 = jnp.float32
BF16 = jnp.bfloat16

D_MODEL = 2048
N_MIXERS = 2

RET_HEADS = 8
RET_QK_DIM = D_MODEL // RET_HEADS
RET_V_DIM = 2 * D_MODEL // RET_HEADS
RET_CHUNK = 256

ATT_HEAD_DIM = 64
ATT_HEADS = D_MODEL // ATT_HEAD_DIM
ATT_KV_HEADS = ATT_HEADS // 8
ATT_GROUP = ATT_HEADS // ATT_KV_HEADS
WINDOW = 128
ATT_BLOCK = 128

D_FF = ((8 * D_MODEL // 3 + 127) // 128) * 128
CONV_WIDTH = 3
EPS = 1e-6

V7X_LANES = 128
V7X_SUBLANES = 8
V7X_VMEM_BYTES = 64 * 1024 * 1024
V7X_COMPILER_SCRATCH_BYTES = 8 * 1024 * 1024
V7X_VMEM_MIN_RESERVE_BYTES = 52 * 1024 * 1024

FF_TILE = 512
D_FF_PAD = ((D_FF + FF_TILE - 1) // FF_TILE) * FF_TILE

D_MODEL_PITCH = D_MODEL + V7X_LANES


def _nbytes(shape, dtype):
    n = 1
    for s in shape:
        n *= s
    return n * jnp.dtype(dtype).itemsize


def _vmem_limit(pipelined, resident=()):
    total = 2 * sum(_nbytes(s, d) for s, d in pipelined)
    total += sum(_nbytes(s, d) for s, d in resident)
    total += V7X_COMPILER_SCRATCH_BYTES
    assert total <= V7X_VMEM_BYTES, total
    return max(total, V7X_VMEM_MIN_RESERVE_BYTES)


def _params(n_grid, vmem_limit):
    return pltpu.CompilerParams(
        dimension_semantics=("arbitrary",) * n_grid, vmem_limit_bytes=vmem_limit)


def _store_padded(ref, rows, value):
    n, d = value.shape
    ref[rows, :d] = value.astype(ref.dtype)
    ref[rows, d:] = jnp.zeros((n, ref.shape[1] - d), ref.dtype)


def _rmsnorm_kernel(x_ref, g_ref, o_ref):
    x = x_ref[...]
    ms = jnp.mean(x * x, axis=-1, keepdims=True)
    _store_padded(o_ref, slice(None), x * lax.rsqrt(ms + EPS) * g_ref[...])


def _rmsnorm(x, gain, *, tm=1024):
    m, d = x.shape
    blocks = [((tm, d), F32), ((1, d), F32), ((tm, D_MODEL_PITCH), BF16)]
    return pl.pallas_call(
        _rmsnorm_kernel,
        grid=(m // tm,),
        in_specs=[pl.BlockSpec((tm, d), lambda i: (i, 0)),
                  pl.BlockSpec((1, d), lambda i: (0, 0))],
        out_specs=pl.BlockSpec((tm, D_MODEL_PITCH), lambda i: (i, 0)),
        out_shape=jax.ShapeDtypeStruct((m, D_MODEL_PITCH), BF16),
        compiler_params=_params(1, _vmem_limit(blocks)),
        name="rmsnorm",
    )(x, gain.reshape(1, d))


def _matmul_bias_kernel(x_ref, w_ref, b_ref, side_ref, o_ref, side_bf_ref, wbf_ref):
    side_bf_ref[...] = side_ref[...].astype(BF16)

    @pl.when(pl.program_id(1) == 0)
    def _():
        wbf_ref[...] = w_ref[...].astype(BF16)

    k = wbf_ref.shape[0]
    acc = jnp.dot(x_ref[:, :k], wbf_ref[...], preferred_element_type=F32) + b_ref[...]
    o_ref[...] = acc.astype(o_ref.dtype)


def _matmul_bias(x, w, bias, layer, side_w, *, tm, tn):
    m, pitch = x.shape
    k, n = w.shape[1:]
    grid = (n // tn, m // tm)
    side_in, side_out, side_shape, side_blocks = _side_cast_specs(side_w, layer, grid)
    blocks = [((tm, pitch), BF16), ((k, tn), F32), ((1, tn), F32), ((tm, tn), BF16)] + side_blocks
    return pl.pallas_call(
        _matmul_bias_kernel,
        grid=grid,
        in_specs=[pl.BlockSpec((tm, pitch), lambda j, i: (i, 0)),
                  pl.BlockSpec((None, k, tn), lambda j, i: (layer, 0, j)),
                  pl.BlockSpec((None, 1, tn), lambda j, i: (layer, 0, j)),
                  side_in],
        out_specs=(pl.BlockSpec((tm, tn), lambda j, i: (i, j)), side_out),
        out_shape=(jax.ShapeDtypeStruct((m, n), BF16), side_shape),
        scratch_shapes=[pltpu.VMEM((k, tn), BF16)],
        compiler_params=_params(2, _vmem_limit(blocks, [((k, tn), BF16)])),
        name="matmul_bias",
    )(x, w, bias.reshape(bias.shape[0], 1, n), side_w)


def _ret_decay_tables():
    hh, dk, cs = RET_HEADS, RET_QK_DIM, RET_CHUNK
    log_gamma = jnp.log1p(-(2.0 ** (-5.0 - jnp.arange(hh, dtype=F32))))
    n = jnp.arange(cs, dtype=F32)[:, None]
    q_scale = jnp.exp(log_gamma[None, :] * (n + 1.0 - cs))
    k_scale = jnp.exp(log_gamma[None, :] * (cs - 1.0 - n)) * (dk ** -0.5)
    table = jnp.concatenate([jnp.repeat(q_scale, dk, axis=1), jnp.repeat(k_scale, dk, axis=1)], axis=1)
    chunk_decay = jnp.exp(log_gamma * cs)
    return table, chunk_decay


RET_PROJ_SUB_ROWS = 512


def _ret_in_proj_kernel(x_ref, w_ref, tab_ref, side_ref, o_ref, side_bf_ref, wbf_ref, *, n_scaled, n_plain):
    side_bf_ref[...] = side_ref[...].astype(BF16)
    j = pl.program_id(0)
    tm = x_ref.shape[0]
    cs = tab_ref.shape[0]

    @pl.when(pl.program_id(1) == 0)
    def _():
        wbf_ref[...] = w_ref[...].astype(BF16)

    def run(epilogue):
        w = wbf_ref[...]
        for r0 in range(0, tm, RET_PROJ_SUB_ROWS):
            acc = jnp.dot(x_ref[r0:r0 + RET_PROJ_SUB_ROWS, :w.shape[0]], w, preferred_element_type=F32)
            o_ref[r0:r0 + RET_PROJ_SUB_ROWS, :] = epilogue(acc, r0).astype(o_ref.dtype)

    def scaled(acc, r0):
        del r0
        rows, cols = acc.shape
        return (acc.reshape(rows // cs, cs, cols) * tab_ref[...][None]).reshape(rows, cols)

    @pl.when(j < n_scaled)
    def _():
        run(scaled)

    @pl.when((j >= n_scaled) & (j < n_scaled + n_plain))
    def _():
        run(lambda acc, r0: acc)

    @pl.when(j >= n_scaled + n_plain)
    def _():
        run(lambda acc, r0: acc / (1.0 + jnp.exp(-acc)))


def _ret_in_proj(x, w, layer, table, side_w, *, tm=2048, tn=1024):
    m, pitch = x.shape
    k, n = w.shape[1:]
    cs, scaled_cols = table.shape
    assert scaled_cols % tn == 0 and RET_PROJ_SUB_ROWS % cs == 0 and tm % RET_PROJ_SUB_ROWS == 0
    n_scaled = scaled_cols // tn
    n_plain = RET_HEADS * RET_V_DIM // tn
    grid = (n // tn, m // tm)
    side_in, side_out, side_shape, side_blocks = _side_cast_specs(side_w, layer, grid)
    blocks = [((tm, pitch), BF16), ((k, tn), F32), ((cs, tn), F32), ((tm, tn), BF16)] + side_blocks
    return pl.pallas_call(
        functools.partial(_ret_in_proj_kernel, n_scaled=n_scaled, n_plain=n_plain),
        grid=grid,
        in_specs=[pl.BlockSpec((tm, pitch), lambda j, i: (i, 0)),
                  pl.BlockSpec((None, k, tn), lambda j, i: (layer, 0, j)),
                  pl.BlockSpec((cs, tn), lambda j, i: (0, jnp.minimum(j, n_scaled - 1))),
                  side_in],
        out_specs=(pl.BlockSpec((tm, tn), lambda j, i: (i, j)), side_out),
        out_shape=(jax.ShapeDtypeStruct((m, n), BF16), side_shape),
        scratch_shapes=[pltpu.VMEM((k, tn), BF16)],
        compiler_params=_params(2, _vmem_limit(blocks, [((k, tn), BF16)])),
        name="ret_in_proj",
    )(x, w, table, side_w)


RET_CHUNKS_PER_STEP = 2


SIDE_CAST_ROWS = 128


def _side_cast_specs(w, layer, grid):
    k, d = w.shape[1:]
    n_chunks = k // SIDE_CAST_ROWS
    n_steps = 1
    for g in grid:
        n_steps *= g
    assert k % SIDE_CAST_ROWS == 0 and n_chunks <= n_steps

    def chunk_of(*idx):
        step = 0
        for g, i in zip(grid, idx):
            step = step * g + i
        return jnp.minimum(step, n_chunks - 1)

    in_spec = pl.BlockSpec((None, SIDE_CAST_ROWS, d), lambda *idx: (layer, chunk_of(*idx), 0))
    out_spec = pl.BlockSpec((SIDE_CAST_ROWS, d), lambda *idx: (chunk_of(*idx), 0))
    blocks = [((SIDE_CAST_ROWS, d), F32), ((SIDE_CAST_ROWS, d), BF16)]
    return in_spec, out_spec, jax.ShapeDtypeStruct((k, d), BF16), blocks


def _retention_kernel(cd_ref, q_ref, k_ref, v_ref, g_ref, y_ref, state_ref, sbf_ref):
    hh, dk, dv, cs = RET_HEADS, RET_QK_DIM, RET_V_DIM, RET_CHUNK

    @pl.when(pl.program_id(0) == 0)
    def _():
        state_ref[...] = jnp.zeros_like(state_ref)
        sbf_ref[...] = jnp.zeros_like(sbf_ref)

    causal = (lax.broadcasted_iota(jnp.int32, (cs, cs), 0)
              >= lax.broadcasted_iota(jnp.int32, (cs, cs), 1))
    for r0 in range(0, q_ref.shape[0], cs):
        rows = slice(r0, r0 + cs)
        for h in range(hh):
            q = q_ref[rows, h * dk:(h + 1) * dk]
            k = k_ref[rows, h * dk:(h + 1) * dk]
            v = v_ref[rows, h * dv:(h + 1) * dv]
            s = lax.dot_general(q, k, (((1,), (1,)), ((), ())), preferred_element_type=F32)
            p = jnp.where(causal, s, 0.0).astype(BF16)
            o = (jnp.dot(p, v, preferred_element_type=F32)
                 + jnp.dot(q, sbf_ref[h], preferred_element_type=F32))
            upd = lax.dot_general(k, v, (((0,), (0,)), ((), ())), preferred_element_type=F32)
            new_state = (state_ref[h] + upd) * cd_ref[h]
            state_ref[h] = new_state
            sbf_ref[h] = new_state.astype(BF16)

            ms = jnp.mean(o * o, axis=-1, keepdims=True)
            gate = g_ref[rows, h * dv:(h + 1) * dv].astype(F32)
            y_ref[rows, h * dv:(h + 1) * dv] = (gate * (o * lax.rsqrt(ms + EPS))).astype(y_ref.dtype)


def _retention(proj, chunk_decay):
    s = proj.shape[0]
    hh, dk, dv, cs = RET_HEADS, RET_QK_DIM, RET_V_DIM, RET_CHUNK
    qk_w, v_w = hh * dk, hh * dv
    assert 2 * qk_w == v_w
    rows = RET_CHUNKS_PER_STEP * cs
    blocks = [((rows, qk_w), BF16)] * 2 + [((rows, v_w), BF16)] * 3
    scratch = [((hh, dk, dv), F32), ((hh, dk, dv), BF16)]
    return pl.pallas_call(
        _retention_kernel,
        grid=(s // rows,),
        in_specs=[pl.BlockSpec(memory_space=pltpu.SMEM),
                  pl.BlockSpec((rows, qk_w), lambda c: (c, 0)),
                  pl.BlockSpec((rows, qk_w), lambda c: (c, 1)),
                  pl.BlockSpec((rows, v_w), lambda c: (c, 1)),
                  pl.BlockSpec((rows, v_w), lambda c: (c, 2))],
        out_specs=pl.BlockSpec((rows, v_w), lambda c: (c, 0)),
        out_shape=jax.ShapeDtypeStruct((s, v_w), BF16),
        scratch_shapes=[pltpu.VMEM((hh, dk, dv), F32), pltpu.VMEM((hh, dk, dv), BF16)],
        compiler_params=_params(1, _vmem_limit(blocks, scratch)),
        name="retention",
    )(chunk_decay, proj, proj, proj, proj)


def _proj_res_norm_kernel(a_ref, w_ref, res_ref, gain_ref, *out_refs, emit_residual):
    h = res_ref[...] + jnp.dot(a_ref[...], w_ref[...], preferred_element_type=F32)
    ms = jnp.mean(h * h, axis=-1, keepdims=True)
    hn = h * lax.rsqrt(ms + EPS) * gain_ref[...]
    if emit_residual:
        h_ref, hn_ref = out_refs
        h_ref[...] = h
        _store_padded(hn_ref, slice(None), hn)
    else:
        (hn_ref,) = out_refs
        hn_ref[...] = hn.astype(hn_ref.dtype)


def _proj_res_norm(a, w, res, gain, *, emit_residual, tm=512):
    m = a.shape[0]
    k, d = w.shape
    row = lambda s: (s, 0)
    if emit_residual:
        out_shape = (jax.ShapeDtypeStruct((m, d), F32), jax.ShapeDtypeStruct((m, D_MODEL_PITCH), BF16))
        out_specs = (pl.BlockSpec((tm, d), row), pl.BlockSpec((tm, D_MODEL_PITCH), row))
        out_blocks = [((tm, d), F32), ((tm, D_MODEL_PITCH), BF16)]
    else:
        out_shape = jax.ShapeDtypeStruct((m, d), F32)
        out_specs = pl.BlockSpec((tm, d), row)
        out_blocks = [((tm, d), F32)]
    blocks = [((tm, k), BF16), ((tm, d), F32), ((1, d), F32)] + out_blocks
    return pl.pallas_call(
        functools.partial(_proj_res_norm_kernel, emit_residual=emit_residual),
        grid=(m // tm,),
        in_specs=[pl.BlockSpec((tm, k), row),
                  pl.BlockSpec((k, d), lambda s: (0, 0), pipeline_mode=pl.Buffered(1)),
                  pl.BlockSpec((tm, d), row),
                  pl.BlockSpec((1, d), lambda s: (0, 0))],
        out_specs=out_specs,
        out_shape=out_shape,
        compiler_params=_params(1, _vmem_limit(blocks, [((k, d), BF16)])),
        name="proj_res_norm" if emit_residual else "proj_res_final_norm",
    )(a, w, res, gain.reshape(1, d))


FF_FULL_TILES = D_FF // FF_TILE
FF_REM = D_FF % FF_TILE
FF_NUM_TILES = FF_FULL_TILES + 1
FF_EPI_ROWS = 32


FF_SUB_ROWS = 256


def _ffn_up_kernel(x_ref, wa_ref, wb0_ref, wb1_ref, cw_ref, cb_ref, wd_ref, g_ref, wd_bf_ref, wbf_ref,
                   *stage_refs):
    wd_bf_ref[...] = wd_ref[...].astype(BF16)
    j = pl.program_id(0)
    i = pl.program_id(1)
    tm = x_ref.shape[0]
    tf = wa_ref.shape[1]
    head = wb0_ref.shape[1]
    hist = V7X_SUBLANES
    sub_rows = FF_SUB_ROWS
    n_sub = tm // sub_rows
    assert n_sub % 2 == 0
    slots = (stage_refs[0:3], stage_refs[3:6])
    last_u1, last_u2 = slots[1][1], slots[1][2]

    @pl.when((i == 0) & (j == 0))
    def _():
        for _, u1_ref, u2_ref in slots:
            u1_ref[sub_rows:, :] = jnp.zeros((hist, 2 * tf), F32)
            u2_ref[sub_rows:, :] = jnp.zeros((hist, 2 * tf), F32)

    @pl.when(i == 0)
    def _():
        wbf_ref[:, :tf] = wa_ref[...].astype(BF16)
        wbf_ref[:, tf:tf + head] = wb0_ref[...].astype(BF16)
        last_u1[sub_rows:, :] = jnp.zeros((hist, 2 * tf), F32)
        last_u2[sub_rows:, :] = jnp.zeros((hist, 2 * tf), F32)

    @pl.when((i == 0) & (j < FF_FULL_TILES))
    def _():
        wbf_ref[:, tf + head:] = wb1_ref[:, :FF_REM].astype(BF16)

    @pl.when((i == 0) & (j == FF_FULL_TILES))
    def _():
        valid = FF_REM - head
        wbf_ref[:, tf + head:tf + head + valid] = wb1_ref[:, :valid].astype(BF16)
        wbf_ref[:, tf + head + valid:] = jnp.zeros((wbf_ref.shape[0], tf - head - valid), BF16)

    def conv(slot, r0, stage_c0, param_c0):
        u0_ref, u1_ref, u2_ref = slot
        rows = slice(r0, r0 + FF_EPI_ROWS)
        cols = slice(stage_c0, stage_c0 + V7X_LANES)
        pcols = slice(param_c0, param_c0 + V7X_LANES)
        sub = V7X_SUBLANES
        tap = [cw_ref[t * sub:(t + 1) * sub, pcols][None] for t in range(CONV_WIDTH)]
        piece = lambda ref: ref[rows, cols].reshape(FF_EPI_ROWS // sub, sub, V7X_LANES)
        c = cb_ref[:, pcols][None] + (tap[0] * piece(u2_ref) + tap[1] * piece(u1_ref) + tap[2] * piece(u0_ref))
        return c.reshape(FF_EPI_ROWS, V7X_LANES)

    def run(width):
        w = wbf_ref[...]
        if width < tf:
            w = jnp.concatenate([w[:, :width], w[:, tf:tf + width]], axis=1)
        for k in range(n_sub):
            s0 = k * sub_rows
            slot = slots[k % 2]
            u0_ref, u1_ref, u2_ref = slot
            _, prev_u1, prev_u2 = slots[(k + 1) % 2]
            u = jnp.dot(x_ref[s0:s0 + sub_rows, :w.shape[0]], w, preferred_element_type=F32)
            live = slice(0, 2 * width)
            u1_ref[0:hist, live] = prev_u1[sub_rows:, live]
            u2_ref[0:hist, live] = prev_u2[sub_rows:, live]
            u0_ref[:, live] = u
            u1_ref[1:1 + sub_rows, live] = u
            u2_ref[2:2 + sub_rows, live] = u
            for r0 in range(0, sub_rows, FF_EPI_ROWS):
                for c0 in range(0, width, V7X_LANES):
                    a = conv(slot, r0, c0, c0)
                    b = conv(slot, r0, width + c0, tf + c0)
                    g_ref[s0 + r0:s0 + r0 + FF_EPI_ROWS, c0:c0 + V7X_LANES] = (
                        a / (1.0 + jnp.exp(-a)) * b).astype(g_ref.dtype)
            if width < tf:
                g_ref[s0:s0 + sub_rows, width:] = jnp.zeros((sub_rows, tf - width), g_ref.dtype)

    @pl.when(j < FF_FULL_TILES)
    def _():
        run(tf)

    @pl.when(j == FF_FULL_TILES)
    def _():
        run(FF_REM)


def _ffn_up(x, w_up, w_down, layer, conv_w, conv_b, *, tm=2048, tf=FF_TILE):
    m, pitch = x.shape
    d = w_up.shape[1]
    nf = FF_NUM_TILES
    head = tf - FF_REM
    assert tf % head == 0 and FF_REM % head == 0
    sub = V7X_SUBLANES
    def per_tile(t):
        taps = t.shape[0]
        t = jnp.pad(t.reshape(taps, 2, D_FF), ((0, 0), (0, 0), (0, D_FF_PAD - D_FF)))
        t = t.reshape(taps, 2, nf, tf).transpose(2, 0, 1, 3).reshape(nf, taps, 1, 2 * tf)
        return jnp.broadcast_to(t, (nf, taps, sub, 2 * tf)).reshape(nf, taps * sub, 2 * tf)
    cw = per_tile(conv_w)
    cb = per_tile(conv_b.reshape(1, -1))
    grid = (nf, m // tm)
    wd_in, wd_out, wd_shape, wd_blocks = _side_cast_specs(w_down, layer, grid)
    blocks = ([((tm, pitch), BF16)] + [((d, tf), F32)] * 2 + [((d, head), F32)]
              + [((CONV_WIDTH * sub, 2 * tf), F32)] + [((sub, 2 * tf), F32)] + [((tm, tf), BF16)]
              + wd_blocks)
    slot = [((FF_SUB_ROWS, 2 * tf), F32)] + [((FF_SUB_ROWS + V7X_SUBLANES, 2 * tf), F32)] * 2
    scratch = [((d, 2 * tf), BF16)] + slot * 2
    return pl.pallas_call(
        _ffn_up_kernel,
        grid=grid,
        in_specs=[pl.BlockSpec((tm, pitch), lambda j, i: (i, 0)),
                  pl.BlockSpec((None, d, tf), lambda j, i: (layer, 0, j)),
                  pl.BlockSpec((None, d, head),
                               lambda j, i: (layer, 0, (FF_FULL_TILES + j) * (tf // head) + FF_REM // head)),
                  pl.BlockSpec((None, d, tf), lambda j, i: (layer, 0, FF_FULL_TILES + 1 + j)),
                  pl.BlockSpec((None, CONV_WIDTH * sub, 2 * tf), lambda j, i: (j, 0, 0)),
                  pl.BlockSpec((None, sub, 2 * tf), lambda j, i: (j, 0, 0)),
                  wd_in],
        out_specs=(pl.BlockSpec((tm, tf), lambda j, i: (i, j)), wd_out),
        out_shape=(jax.ShapeDtypeStruct((m, D_FF_PAD), BF16), wd_shape),
        scratch_shapes=[pltpu.VMEM(*shape_dtype) for shape_dtype in scratch],
        compiler_params=_params(2, _vmem_limit(blocks, scratch)),
        name="ffn_up",
    )(x, w_up, w_up, w_up, cw, cb, w_down)


SWA_BLOCKS_PER_STEP = 1


def _swa_kernel(slopes_ref, sinks_ref, q_ref, kp_ref, kc_ref, vp_ref, vc_ref, o_ref, bias_ref):
    n = pl.program_id(0)
    blk, dh, grp = ATT_BLOCK, ATT_HEAD_DIM, ATT_GROUP
    lanes = 2 * dh
    pairs = grp // 2

    def fill_bias(table, first_block):
        qi = lax.broadcasted_iota(jnp.int32, (blk, 2 * blk), 0)
        kj = lax.broadcasted_iota(jnp.int32, (blk, 2 * blk), 1)
        dist = blk + qi - kj
        valid = (dist >= 0) & (dist < WINDOW)
        if first_block:
            valid = valid & (kj >= blk)
        dist_f = dist.astype(F32)
        for head in range(ATT_HEADS):
            bias_ref[table, head] = jnp.where(valid, -(slopes_ref[head] * dist_f), -jnp.inf)

    @pl.when(n == 0)
    def _():
        fill_bias(0, True)
        for table in range(1, SWA_BLOCKS_PER_STEP):
            fill_bias(table, False)

    @pl.when(n == 1)
    def _():
        fill_bias(0, False)

    lane_kv = lax.broadcasted_iota(jnp.int32, (2 * blk, lanes), 1)
    lane_o = lax.broadcasted_iota(jnp.int32, (blk, lanes), 1)

    for b in range(SWA_BLOCKS_PER_STEP):
        rows = slice(b * blk, (b + 1) * blk)
        prev_rows = slice((b - 1) * blk, b * blk)
        for pair in range(ATT_KV_HEADS // 2):
            cols = slice(pair * lanes, (pair + 1) * lanes)
            k_prev = kp_ref[:, cols] if b == 0 else kc_ref[prev_rows, cols]
            v_prev = vp_ref[:, cols] if b == 0 else vc_ref[prev_rows, cols]
            k2 = jnp.concatenate([k_prev, kc_ref[rows, cols]], axis=0).astype(F32) * (dh ** -0.5)
            v2 = jnp.concatenate([v_prev, vc_ref[rows, cols]], axis=0).astype(F32)
            k2s = pltpu.roll(k2, dh, axis=1)
            v2s = pltpu.roll(v2, dh, axis=1)
            for sub in range(2):
                kvh = 2 * pair + sub
                k_lo, k_hi = (k2, k2s) if sub == 0 else (k2s, k2)
                v_lo, v_hi = (v2, v2s) if sub == 0 else (v2s, v2)
                kz = (jnp.where(lane_kv < dh, k_lo, 0.0).astype(BF16),
                      jnp.where(lane_kv >= dh, k_hi, 0.0).astype(BF16))
                vz = (jnp.where(lane_kv < dh, v_lo, 0.0).astype(BF16),
                      jnp.where(lane_kv >= dh, v_hi, 0.0).astype(BF16))
                qcol0 = kvh * grp * dh
                q_stack = jnp.concatenate(
                    [q_ref[rows, qcol0 + hp * lanes:qcol0 + (hp + 1) * lanes] for hp in range(pairs)], axis=0)
                acc = None
                inv = [[None, None] for _ in range(pairs)]
                for par in range(2):
                    s_all = lax.dot_general(q_stack, kz[par], (((1,), (1,)), ((), ())),
                                            preferred_element_type=F32)
                    e_parts = []
                    for hp in range(pairs):
                        head = kvh * grp + 2 * hp + par
                        s = s_all[hp * blk:(hp + 1) * blk] + bias_ref[b, head]
                        sink = sinks_ref[head]
                        m = jnp.maximum(jnp.max(s, axis=-1, keepdims=True), sink)
                        e = jnp.exp(s - m)
                        denom = jnp.sum(e, axis=-1, keepdims=True) + jnp.exp(sink - m)
                        inv[hp][par] = 1.0 / denom
                        e_parts.append(e.astype(BF16))
                    pv = jnp.dot(jnp.concatenate(e_parts, axis=0), vz[par], preferred_element_type=F32)
                    acc = pv if acc is None else acc + pv
                for hp in range(pairs):
                    scale = jnp.where(lane_o < dh, inv[hp][0], inv[hp][1])
                    o_ref[rows, qcol0 + hp * lanes:qcol0 + (hp + 1) * lanes] = (
                        acc[hp * blk:(hp + 1) * blk] * scale).astype(o_ref.dtype)


def _swa(qkv, sinks):
    s = qkv.shape[0]
    blk = ATT_BLOCK
    dq = ATT_HEADS * ATT_HEAD_DIM
    dkv = ATT_KV_HEADS * ATT_HEAD_DIM
    k_blk = dq // dkv
    v_blk = k_blk + 1
    slopes = 2.0 ** (-8.0 * jnp.arange(1, ATT_HEADS + 1, dtype=F32) / ATT_HEADS)
    per_step = SWA_BLOCKS_PER_STEP
    rows = per_step * blk
    prev = lambda n: jnp.maximum(n * per_step - 1, 0)
    blocks = [((rows, dq), BF16)] * 2 + [((blk, dkv), BF16)] * 2 + [((rows, dkv), BF16)] * 2
    bias = ((per_step, ATT_HEADS, blk, 2 * blk), F32)
    return pl.pallas_call(
        _swa_kernel,
        grid=(s // rows,),
        in_specs=[pl.BlockSpec(memory_space=pltpu.SMEM),
                  pl.BlockSpec(memory_space=pltpu.SMEM),
                  pl.BlockSpec((rows, dq), lambda n: (n, 0)),
                  pl.BlockSpec((blk, dkv), lambda n: (prev(n), k_blk)),
                  pl.BlockSpec((rows, dkv), lambda n: (n, k_blk)),
                  pl.BlockSpec((blk, dkv), lambda n: (prev(n), v_blk)),
                  pl.BlockSpec((rows, dkv), lambda n: (n, v_blk))],
        out_specs=pl.BlockSpec((rows, dq), lambda n: (n, 0)),
        out_shape=jax.ShapeDtypeStruct((s, dq), BF16),
        scratch_shapes=[pltpu.VMEM(*bias)],
        compiler_params=_params(1, _vmem_limit(blocks, [bias])),
        name="swa",
    )(slopes, sinks.astype(F32), qkv, qkv, qkv, qkv, qkv)


def kernel(x, norm_mix_g, ret_w_in, ret_w_out, att_w_qkv, att_b_qkv, att_sinks, att_w_out,
           norm_ffn_g, ffn_w_up, ffn_conv_w, ffn_conv_b, ffn_w_down, final_norm_g):
    batch, seq, d = x.shape
    assert batch == 1 and d == D_MODEL
    depth = norm_mix_g.shape[0]

    h = x.reshape(seq, d)
    hn = _rmsnorm(h, norm_mix_g[0])
    for i in range(depth):
        j = i // N_MIXERS
        if i % N_MIXERS == 0:
            decay_table, chunk_decay = _ret_decay_tables()
            proj, w_out = _ret_in_proj(hn, ret_w_in, j, decay_table, ret_w_out)
            mixed = _retention(proj, chunk_decay)
        else:
            qkv, w_out = _matmul_bias(hn, att_w_qkv, att_b_qkv, j, att_w_out, tm=1024, tn=1280)
            mixed = _swa(qkv, att_sinks[j])
        h, hn = _proj_res_norm(mixed, w_out, h, norm_ffn_g[i], emit_residual=True)

        gated, w_down = _ffn_up(hn, ffn_w_up, ffn_w_down, i, ffn_conv_w[i], ffn_conv_b[i])
        if i + 1 < depth:
            h, hn = _proj_res_norm(gated, w_down, h, norm_mix_g[i + 1], emit_residual=True)
        else:
            out = _proj_res_norm(gated, w_down, h, final_norm_g, emit_residual=False)
    return out.reshape(batch, seq, d)
```

```python
import functools

import jax
import jax.numpy as jnp
from jax import lax
from jax.experimental import pallas as pl
from jax.experimental.pallas import tpu as pltpu

F32 = jnp.float32
BF16 = jnp.bfloat16

D_MODEL = 2048
N_MIXERS = 2

RET_HEADS = 8
RET_QK_DIM = D_MODEL // RET_HEADS
RET_V_DIM = 2 * D_MODEL // RET_HEADS
RET_CHUNK = 256

ATT_HEAD_DIM = 64
ATT_HEADS = D_MODEL // ATT_HEAD_DIM
ATT_KV_HEADS = ATT_HEADS // 8
ATT_GROUP = ATT_HEADS // ATT_KV_HEADS
WINDOW = 128
ATT_BLOCK = 128

D_FF = ((8 * D_MODEL // 3 + 127) // 128) * 128
CONV_WIDTH = 3
EPS = 1e-6

V7X_LANES = 128
V7X_SUBLANES = 8
V7X_VMEM_BYTES = 64 * 1024 * 1024
V7X_COMPILER_SCRATCH_BYTES = 8 * 1024 * 1024
V7X_VMEM_MIN_RESERVE_BYTES = 52 * 1024 * 1024

FF_TILE = 512
D_FF_PAD = ((D_FF + FF_TILE - 1) // FF_TILE) * FF_TILE

D_MODEL_PITCH = D_MODEL + V7X_LANES


def _nbytes(shape, dtype):
    n = 1
    for s in shape:
        n *= s
    return n * jnp.dtype(dtype).itemsize


def _vmem_limit(pipelined, resident=()):
    total = 2 * sum(_nbytes(s, d) for s, d in pipelined)
    total += sum(_nbytes(s, d) for s, d in resident)
    total += V7X_COMPILER_SCRATCH_BYTES
    assert total <= V7X_VMEM_BYTES, total
    return max(total, V7X_VMEM_MIN_RESERVE_BYTES)


def _params(n_grid, vmem_limit):
    return pltpu.CompilerParams(
        dimension_semantics=("arbitrary",) * n_grid, vmem_limit_bytes=vmem_limit)


def _store_padded(ref, rows, value):
    n, d = value.shape
    ref[rows, :d] = value.astype(ref.dtype)
    ref[rows, d:] = jnp.zeros((n, ref.shape[1] - d), ref.dtype)


def _rmsnorm_kernel(x_ref, g_ref, o_ref):
    x = x_ref[...]
    ms = jnp.mean(x * x, axis=-1, keepdims=True)
    _store_padded(o_ref, slice(None), x * lax.rsqrt(ms + EPS) * g_ref[...])


def _rmsnorm(x, gain, *, tm=1024):
    m, d = x.shape
    blocks = [((tm, d), F32), ((1, d), F32), ((tm, D_MODEL_PITCH), BF16)]
    return pl.pallas_call(
        _rmsnorm_kernel,
        grid=(m // tm,),
        in_specs=[pl.BlockSpec((tm, d), lambda i: (i, 0)),
                  pl.BlockSpec((1, d), lambda i: (0, 0))],
        out_specs=pl.BlockSpec((tm, D_MODEL_PITCH), lambda i: (i, 0)),
        out_shape=jax.ShapeDtypeStruct((m, D_MODEL_PITCH), BF16),
        compiler_params=_params(1, _vmem_limit(blocks)),
        name="rmsnorm",
    )(x, gain.reshape(1, d))


def _matmul_bias_kernel(x_ref, w_ref, b_ref, side_ref, o_ref, side_bf_ref, wbf_ref):
    side_bf_ref[...] = side_ref[...].astype(BF16)

    @pl.when(pl.program_id(1) == 0)
    def _():
        wbf_ref[...] = w_ref[...].astype(BF16)

    k = wbf_ref.shape[0]
    acc = jnp.dot(x_ref[:, :k], wbf_ref[...], preferred_element_type=F32) + b_ref[...]
    o_ref[...] = acc.astype(o_ref.dtype)


def _matmul_bias(x, w, bias, layer, side_w, *, tm, tn):
    m, pitch = x.shape
    k, n = w.shape[1:]
    grid = (n // tn, m // tm)
    side_in, side_out, side_shape, side_blocks = _side_cast_specs(side_w, layer, grid)
    blocks = [((tm, pitch), BF16), ((k, tn), F32), ((1, tn), F32), ((tm, tn), BF16)] + side_blocks
    return pl.pallas_call(
        _matmul_bias_kernel,
        grid=grid,
        in_specs=[pl.BlockSpec((tm, pitch), lambda j, i: (i, 0)),
                  pl.BlockSpec((None, k, tn), lambda j, i: (layer, 0, j)),
                  pl.BlockSpec((None, 1, tn), lambda j, i: (layer, 0, j)),
                  side_in],
        out_specs=(pl.BlockSpec((tm, tn), lambda j, i: (i, j)), side_out),
        out_shape=(jax.ShapeDtypeStruct((m, n), BF16), side_shape),
        scratch_shapes=[pltpu.VMEM((k, tn), BF16)],
        compiler_params=_params(2, _vmem_limit(blocks, [((k, tn), BF16)])),
        name="matmul_bias",
    )(x, w, bias.reshape(bias.shape[0], 1, n), side_w)


def _ret_decay_tables():
    hh, dk, cs = RET_HEADS, RET_QK_DIM, RET_CHUNK
    log_gamma = jnp.log1p(-(2.0 ** (-5.0 - jnp.arange(hh, dtype=F32))))
    n = jnp.arange(cs, dtype=F32)[:, None]
    q_scale = jnp.exp(log_gamma[None, :] * (n + 1.0 - cs))
    k_scale = jnp.exp(log_gamma[None, :] * (cs - 1.0 - n)) * (dk ** -0.5)
    table = jnp.concatenate([jnp.repeat(q_scale, dk, axis=1), jnp.repeat(k_scale, dk, axis=1)], axis=1)
    chunk_decay = jnp.exp(log_gamma * cs)
    return table, chunk_decay


RET_PROJ_SUB_ROWS = 512


def _ret_in_proj_kernel(x_ref, w_ref, tab_ref, side_ref, o_ref, side_bf_ref, wbf_ref, *, n_scaled, n_plain):
    side_bf_ref[...] = side_ref[...].astype(BF16)
    j = pl.program_id(0)
    tm = x_ref.shape[0]
    cs = tab_ref.shape[0]

    @pl.when(pl.program_id(1) == 0)
    def _():
        wbf_ref[...] = w_ref[...].astype(BF16)

    def run(epilogue):
        w = wbf_ref[...]
        for r0 in range(0, tm, RET_PROJ_SUB_ROWS):
            acc = jnp.dot(x_ref[r0:r0 + RET_PROJ_SUB_ROWS, :w.shape[0]], w, preferred_element_type=F32)
            o_ref[r0:r0 + RET_PROJ_SUB_ROWS, :] = epilogue(acc, r0).astype(o_ref.dtype)

    def scaled(acc, r0):
        del r0
        rows, cols = acc.shape
        return (acc.reshape(rows // cs, cs, cols) * tab_ref[...][None]).reshape(rows, cols)

    @pl.when(j < n_scaled)
    def _():
        run(scaled)

    @pl.when((j >= n_scaled) & (j < n_scaled + n_plain))
    def _():
        run(lambda acc, r0: acc)

    @pl.when(j >= n_scaled + n_plain)
    def _():
        run(lambda acc, r0: acc / (1.0 + jnp.exp(-acc)))


def _ret_in_proj(x, w, layer, table, side_w, *, tm=2048, tn=1024):
    m, pitch = x.shape
    k, n = w.shape[1:]
    cs, scaled_cols = table.shape
    assert scaled_cols % tn == 0 and RET_PROJ_SUB_ROWS % cs == 0 and tm % RET_PROJ_SUB_ROWS == 0
    n_scaled = scaled_cols // tn
    n_plain = RET_HEADS * RET_V_DIM // tn
    grid = (n // tn, m // tm)
    side_in, side_out, side_shape, side_blocks = _side_cast_specs(side_w, layer, grid)
    blocks = [((tm, pitch), BF16), ((k, tn), F32), ((cs, tn), F32), ((tm, tn), BF16)] + side_blocks
    return pl.pallas_call(
        functools.partial(_ret_in_proj_kernel, n_scaled=n_scaled, n_plain=n_plain),
        grid=grid,
        in_specs=[pl.BlockSpec((tm, pitch), lambda j, i: (i, 0)),
                  pl.BlockSpec((None, k, tn), lambda j, i: (layer, 0, j)),
                  pl.BlockSpec((cs, tn), lambda j, i: (0, jnp.minimum(j, n_scaled - 1))),
                  side_in],
        out_specs=(pl.BlockSpec((tm, tn), lambda j, i: (i, j)), side_out),
        out_shape=(jax.ShapeDtypeStruct((m, n), BF16), side_shape),
        scratch_shapes=[pltpu.VMEM((k, tn), BF16)],
        compiler_params=_params(2, _vmem_limit(blocks, [((k, tn), BF16)])),
        name="ret_in_proj",
    )(x, w, table, side_w)


RET_CHUNKS_PER_STEP = 2


SIDE_CAST_ROWS = 128


def _side_cast_specs(w, layer, grid):
    k, d = w.shape[1:]
    n_chunks = k // SIDE_CAST_ROWS
    n_steps = 1
    for g in grid:
        n_steps *= g
    assert k % SIDE_CAST_ROWS == 0 and n_chunks <= n_steps

    def chunk_of(*idx):
        step = 0
        for g, i in zip(grid, idx):
            step = step * g + i
        return jnp.minimum(step, n_chunks - 1)

    in_spec = pl.BlockSpec((None, SIDE_CAST_ROWS, d), lambda *idx: (layer, chunk_of(*idx), 0))
    out_spec = pl.BlockSpec((SIDE_CAST_ROWS, d), lambda *idx: (chunk_of(*idx), 0))
    blocks = [((SIDE_CAST_ROWS, d), F32), ((SIDE_CAST_ROWS, d), BF16)]
    return in_spec, out_spec, jax.ShapeDtypeStruct((k, d), BF16), blocks


def _retention_kernel(cd_ref, q_ref, k_ref, v_ref, g_ref, y_ref, state_ref, sbf_ref):
    hh, dk, dv, cs = RET_HEADS, RET_QK_DIM, RET_V_DIM, RET_CHUNK

    @pl.when(pl.program_id(0) == 0)
    def _():
        state_ref[...] = jnp.zeros_like(state_ref)
        sbf_ref[...] = jnp.zeros_like(sbf_ref)

    causal = (lax.broadcasted_iota(jnp.int32, (cs, cs), 0)
              >= lax.broadcasted_iota(jnp.int32, (cs, cs), 1))
    for r0 in range(0, q_ref.shape[0], cs):
        rows = slice(r0, r0 + cs)
        for h in range(hh):
            q = q_ref[rows, h * dk:(h + 1) * dk]
            k = k_ref[rows, h * dk:(h + 1) * dk]
            v = v_ref[rows, h * dv:(h + 1) * dv]
            s = lax.dot_general(q, k, (((1,), (1,)), ((), ())), preferred_element_type=F32)
            p = jnp.where(causal, s, 0.0).astype(BF16)
            o = (jnp.dot(p, v, preferred_element_type=F32)
                 + jnp.dot(q, sbf_ref[h], preferred_element_type=F32))
            upd = lax.dot_general(k, v, (((0,), (0,)), ((), ())), preferred_element_type=F32)
            new_state = (state_ref[h] + upd) * cd_ref[h]
            state_ref[h] = new_state
            sbf_ref[h] = new_state.astype(BF16)

            ms = jnp.mean(o * o, axis=-1, keepdims=True)
            gate = g_ref[rows, h * dv:(h + 1) * dv].astype(F32)
            y_ref[rows, h * dv:(h + 1) * dv] = (gate * (o * lax.rsqrt(ms + EPS))).astype(y_ref.dtype)


def _retention(proj, chunk_decay):
    s = proj.shape[0]
    hh, dk, dv, cs = RET_HEADS, RET_QK_DIM, RET_V_DIM, RET_CHUNK
    qk_w, v_w = hh * dk, hh * dv
    assert 2 * qk_w == v_w
    rows = RET_CHUNKS_PER_STEP * cs
    blocks = [((rows, qk_w), BF16)] * 2 + [((rows, v_w), BF16)] * 3
    scratch = [((hh, dk, dv), F32), ((hh, dk, dv), BF16)]
    return pl.pallas_call(
        _retention_kernel,
        grid=(s // rows,),
        in_specs=[pl.BlockSpec(memory_space=pltpu.SMEM),
                  pl.BlockSpec((rows, qk_w), lambda c: (c, 0)),
                  pl.BlockSpec((rows, qk_w), lambda c: (c, 1)),
                  pl.BlockSpec((rows, v_w), lambda c: (c, 1)),
                  pl.BlockSpec((rows, v_w), lambda c: (c, 2))],
        out_specs=pl.BlockSpec((rows, v_w), lambda c: (c, 0)),
        out_shape=jax.ShapeDtypeStruct((s, v_w), BF16),
        scratch_shapes=[pltpu.VMEM((hh, dk, dv), F32), pltpu.VMEM((hh, dk, dv), BF16)],
        compiler_params=_params(1, _vmem_limit(blocks, scratch)),
        name="retention",
    )(chunk_decay, proj, proj, proj, proj)


def _proj_res_norm_kernel(a_ref, w_ref, res_ref, gain_ref, *out_refs, emit_residual):
    h = res_ref[...] + jnp.dot(a_ref[...], w_ref[...], preferred_element_type=F32)
    ms = jnp.mean(h * h, axis=-1, keepdims=True)
    hn = h * lax.rsqrt(ms + EPS) * gain_ref[...]
    if emit_residual:
        h_ref, hn_ref = out_refs
        h_ref[...] = h
        _store_padded(hn_ref, slice(None), hn)
    else:
        (hn_ref,) = out_refs
        hn_ref[...] = hn.astype(hn_ref.dtype)


def _proj_res_norm(a, w, res, gain, *, emit_residual, tm=512):
    m = a.shape[0]
    k, d = w.shape
    row = lambda s: (s, 0)
    if emit_residual:
        out_shape = (jax.ShapeDtypeStruct((m, d), F32), jax.ShapeDtypeStruct((m, D_MODEL_PITCH), BF16))
        out_specs = (pl.BlockSpec((tm, d), row), pl.BlockSpec((tm, D_MODEL_PITCH), row))
        out_blocks = [((tm, d), F32), ((tm, D_MODEL_PITCH), BF16)]
    else:
        out_shape = jax.ShapeDtypeStruct((m, d), F32)
        out_specs = pl.BlockSpec((tm, d), row)
        out_blocks = [((tm, d), F32)]
    blocks = [((tm, k), BF16), ((tm, d), F32), ((1, d), F32)] + out_blocks
    return pl.pallas_call(
        functools.partial(_proj_res_norm_kernel, emit_residual=emit_residual),
        grid=(m // tm,),
        in_specs=[pl.BlockSpec((tm, k), row),
                  pl.BlockSpec((k, d), lambda s: (0, 0), pipeline_mode=pl.Buffered(1)),
                  pl.BlockSpec((tm, d), row),
                  pl.BlockSpec((1, d), lambda s: (0, 0))],
        out_specs=out_specs,
        out_shape=out_shape,
        compiler_params=_params(1, _vmem_limit(blocks, [((k, d), BF16)])),
        name="proj_res_norm" if emit_residual else "proj_res_final_norm",
    )(a, w, res, gain.reshape(1, d))


FF_FULL_TILES = D_FF // FF_TILE
FF_REM = D_FF % FF_TILE
FF_NUM_TILES = FF_FULL_TILES + 1
FF_EPI_ROWS = 32


FF_SUB_ROWS = 256


def _ffn_up_kernel(x_ref, wa_ref, wb0_ref, wb1_ref, cw_ref, cb_ref, wd_ref, g_ref, wd_bf_ref, wbf_ref,
                   *stage_refs):
    wd_bf_ref[...] = wd_ref[...].astype(BF16)
    j = pl.program_id(0)
    i = pl.program_id(1)
    tm = x_ref.shape[0]
    tf = wa_ref.shape[1]
    head = wb0_ref.shape[1]
    hist = V7X_SUBLANES
    sub_rows = FF_SUB_ROWS
    n_sub = tm // sub_rows
    assert n_sub % 2 == 0
    slots = (stage_refs[0:3], stage_refs[3:6])
    last_u1, last_u2 = slots[1][1], slots[1][2]

    @pl.when((i == 0) & (j == 0))
    def _():
        for _, u1_ref, u2_ref in slots:
            u1_ref[sub_rows:, :] = jnp.zeros((hist, 2 * tf), F32)
            u2_ref[sub_rows:, :] = jnp.zeros((hist, 2 * tf), F32)

    @pl.when(i == 0)
    def _():
        wbf_ref[:, :tf] = wa_ref[...].astype(BF16)
        wbf_ref[:, tf:tf + head] = wb0_ref[...].astype(BF16)
        last_u1[sub_rows:, :] = jnp.zeros((hist, 2 * tf), F32)
        last_u2[sub_rows:, :] = jnp.zeros((hist, 2 * tf), F32)

    @pl.when((i == 0) & (j < FF_FULL_TILES))
    def _():
        wbf_ref[:, tf + head:] = wb1_ref[:, :FF_REM].astype(BF16)

    @pl.when((i == 0) & (j == FF_FULL_TILES))
    def _():
        valid = FF_REM - head
        wbf_ref[:, tf + head:tf + head + valid] = wb1_ref[:, :valid].astype(BF16)
        wbf_ref[:, tf + head + valid:] = jnp.zeros((wbf_ref.shape[0], tf - head - valid), BF16)

    def conv(slot, r0, stage_c0, param_c0):
        u0_ref, u1_ref, u2_ref = slot
        rows = slice(r0, r0 + FF_EPI_ROWS)
        cols = slice(stage_c0, stage_c0 + V7X_LANES)
        pcols = slice(param_c0, param_c0 + V7X_LANES)
        sub = V7X_SUBLANES
        tap = [cw_ref[t * sub:(t + 1) * sub, pcols][None] for t in range(CONV_WIDTH)]
        piece = lambda ref: ref[rows, cols].reshape(FF_EPI_ROWS // sub, sub, V7X_LANES)
        c = cb_ref[:, pcols][None] + (tap[0] * piece(u2_ref) + tap[1] * piece(u1_ref) + tap[2] * piece(u0_ref))
        return c.reshape(FF_EPI_ROWS, V7X_LANES)

    def run(width):
        w = wbf_ref[...]
        if width < tf:
            w = jnp.concatenate([w[:, :width], w[:, tf:tf + width]], axis=1)
        for k in range(n_sub):
            s0 = k * sub_rows
            slot = slots[k % 2]
            u0_ref, u1_ref, u2_ref = slot
            _, prev_u1, prev_u2 = slots[(k + 1) % 2]
            u = jnp.dot(x_ref[s0:s0 + sub_rows, :w.shape[0]], w, preferred_element_type=F32)
            live = slice(0, 2 * width)
            u1_ref[0:hist, live] = prev_u1[sub_rows:, live]
            u2_ref[0:hist, live] = prev_u2[sub_rows:, live]
            u0_ref[:, live] = u
            u1_ref[1:1 + sub_rows, live] = u
            u2_ref[2:2 + sub_rows, live] = u
            for r0 in range(0, sub_rows, FF_EPI_ROWS):
                for c0 in range(0, width, V7X_LANES):
                    a = conv(slot, r0, c0, c0)
                    b = conv(slot, r0, width + c0, tf + c0)
                    g_ref[s0 + r0:s0 + r0 + FF_EPI_ROWS, c0:c0 + V7X_LANES] = (
                        a / (1.0 + jnp.exp(-a)) * b).astype(g_ref.dtype)
            if width < tf:
                g_ref[s0:s0 + sub_rows, width:] = jnp.zeros((sub_rows, tf - width), g_ref.dtype)

    @pl.when(j < FF_FULL_TILES)
    def _():
        run(tf)

    @pl.when(j == FF_FULL_TILES)
    def _():
        run(FF_REM)


def _ffn_up(x, w_up, w_down, layer, conv_w, conv_b, *, tm=2048, tf=FF_TILE):
    m, pitch = x.shape
    d = w_up.shape[1]
    nf = FF_NUM_TILES
    head = tf - FF_REM
    assert tf % head == 0 and FF_REM % head == 0
    sub = V7X_SUBLANES
    def per_tile(t):
        taps = t.shape[0]
        t = jnp.pad(t.reshape(taps, 2, D_FF), ((0, 0), (0, 0), (0, D_FF_PAD - D_FF)))
        t = t.reshape(taps, 2, nf, tf).transpose(2, 0, 1, 3).reshape(nf, taps, 1, 2 * tf)
        return jnp.broadcast_to(t, (nf, taps, sub, 2 * tf)).reshape(nf, taps * sub, 2 * tf)
    cw = per_tile(conv_w)
    cb = per_tile(conv_b.reshape(1, -1))
    grid = (nf, m // tm)
    wd_in, wd_out, wd_shape, wd_blocks = _side_cast_specs(w_down, layer, grid)
    blocks = ([((tm, pitch), BF16)] + [((d, tf), F32)] * 2 + [((d, head), F32)]
              + [((CONV_WIDTH * sub, 2 * tf), F32)] + [((sub, 2 * tf), F32)] + [((tm, tf), BF16)]
              + wd_blocks)
    slot = [((FF_SUB_ROWS, 2 * tf), F32)] + [((FF_SUB_ROWS + V7X_SUBLANES, 2 * tf), F32)] * 2
    scratch = [((d, 2 * tf), BF16)] + slot * 2
    return pl.pallas_call(
        _ffn_up_kernel,
        grid=grid,
        in_specs=[pl.BlockSpec((tm, pitch), lambda j, i: (i, 0)),
                  pl.BlockSpec((None, d, tf), lambda j, i: (layer, 0, j)),
                  pl.BlockSpec((None, d, head),
                               lambda j, i: (layer, 0, (FF_FULL_TILES + j) * (tf // head) + FF_REM // head)),
                  pl.BlockSpec((None, d, tf), lambda j, i: (layer, 0, FF_FULL_TILES + 1 + j)),
                  pl.BlockSpec((None, CONV_WIDTH * sub, 2 * tf), lambda j, i: (j, 0, 0)),
                  pl.BlockSpec((None, sub, 2 * tf), lambda j, i: (j, 0, 0)),
                  wd_in],
        out_specs=(pl.BlockSpec((tm, tf), lambda j, i: (i, j)), wd_out),
        out_shape=(jax.ShapeDtypeStruct((m, D_FF_PAD), BF16), wd_shape),
        scratch_shapes=[pltpu.VMEM(*shape_dtype) for shape_dtype in scratch],
        compiler_params=_params(2, _vmem_limit(blocks, scratch)),
        name="ffn_up",
    )(x, w_up, w_up, w_up, cw, cb, w_down)


SWA_BLOCKS_PER_STEP = 1


def _swa_kernel(slopes_ref, sinks_ref, q_ref, kp_ref, kc_ref, vp_ref, vc_ref, o_ref, bias_ref):
    n = pl.program_id(0)
    blk, dh, grp = ATT_BLOCK, ATT_HEAD_DIM, ATT_GROUP
    lanes = 2 * dh
    pairs = grp // 2

    qi = lax.broadcasted_iota(jnp.int32, (blk, blk), 0)
    kj = lax.broadcasted_iota(jnp.int32, (blk, blk), 1)
    from_prev = kj > qi

    def fill_bias(table, first_block):
        dist_f = jnp.where(from_prev, blk + qi - kj, qi - kj).astype(F32)
        assert WINDOW == blk
        for head in range(ATT_HEADS):
            bias = -(slopes_ref[head] * dist_f)
            bias_ref[table, head] = jnp.where(from_prev, -jnp.inf, bias) if first_block else bias

    @pl.when(n == 0)
    def _():
        fill_bias(0, True)
        for table in range(1, SWA_BLOCKS_PER_STEP):
            fill_bias(table, False)

    @pl.when(n == 1)
    def _():
        fill_bias(0, False)

    lane_kv = lax.broadcasted_iota(jnp.int32, (2 * blk, lanes), 1)
    lane_o = lax.broadcasted_iota(jnp.int32, (blk, lanes), 1)

    for b in range(SWA_BLOCKS_PER_STEP):
        rows = slice(b * blk, (b + 1) * blk)
        prev_rows = slice((b - 1) * blk, b * blk)
        for pair in range(ATT_KV_HEADS // 2):
            cols = slice(pair * lanes, (pair + 1) * lanes)
            k_prev = kp_ref[:, cols] if b == 0 else kc_ref[prev_rows, cols]
            v_prev = vp_ref[:, cols] if b == 0 else vc_ref[prev_rows, cols]
            k2 = jnp.concatenate([k_prev, kc_ref[rows, cols]], axis=0).astype(F32) * (dh ** -0.5)
            v2 = jnp.concatenate([v_prev, vc_ref[rows, cols]], axis=0).astype(F32)
            k2s = pltpu.roll(k2, dh, axis=1)
            v2s = pltpu.roll(v2, dh, axis=1)
            for sub in range(2):
                kvh = 2 * pair + sub
                k_lo, k_hi = (k2, k2s) if sub == 0 else (k2s, k2)
                v_lo, v_hi = (v2, v2s) if sub == 0 else (v2s, v2)
                kz = (jnp.where(lane_kv < dh, k_lo, 0.0).astype(BF16),
                      jnp.where(lane_kv >= dh, k_hi, 0.0).astype(BF16))
                vz = (jnp.where(lane_kv < dh, v_lo, 0.0).astype(BF16),
                      jnp.where(lane_kv >= dh, v_hi, 0.0).astype(BF16))
                qcol0 = kvh * grp * dh
                q_stack = jnp.concatenate(
                    [q_ref[rows, qcol0 + hp * lanes:qcol0 + (hp + 1) * lanes] for hp in range(pairs)], axis=0)
                acc = None
                inv = [[None, None] for _ in range(pairs)]
                for par in range(2):
                    s_all = lax.dot_general(q_stack, kz[par], (((1,), (1,)), ((), ())),
                                            preferred_element_type=F32)
                    e_parts = []
                    for hp in range(pairs):
                        head = kvh * grp + 2 * hp + par
                        s2 = s_all[hp * blk:(hp + 1) * blk]
                        s = jnp.where(from_prev, s2[:, :blk], s2[:, blk:]) + bias_ref[b, head]
                        sink = sinks_ref[head]
                        m = jnp.maximum(jnp.max(s, axis=-1, keepdims=True), sink)
                        e = jnp.exp(s - m)
                        denom = jnp.sum(e, axis=-1, keepdims=True) + jnp.exp(sink - m)
                        inv[hp][par] = 1.0 / denom
                        e = e.astype(BF16)
                        zero = jnp.zeros_like(e)
                        e_parts.append(jnp.concatenate(
                            [jnp.where(from_prev, e, zero), jnp.where(from_prev, zero, e)], axis=1))
                    pv = jnp.dot(jnp.concatenate(e_parts, axis=0), vz[par], preferred_element_type=F32)
                    acc = pv if acc is None else acc + pv
                for hp in range(pairs):
                    scale = jnp.where(lane_o < dh, inv[hp][0], inv[hp][1])
                    o_ref[rows, qcol0 + hp * lanes:qcol0 + (hp + 1) * lanes] = (
                        acc[hp * blk:(hp + 1) * blk] * scale).astype(o_ref.dtype)


def _swa(qkv, sinks):
    s = qkv.shape[0]
    blk = ATT_BLOCK
    dq = ATT_HEADS * ATT_HEAD_DIM
    dkv = ATT_KV_HEADS * ATT_HEAD_DIM
    k_blk = dq // dkv
    v_blk = k_blk + 1
    slopes = 2.0 ** (-8.0 * jnp.arange(1, ATT_HEADS + 1, dtype=F32) / ATT_HEADS)
    per_step = SWA_BLOCKS_PER_STEP
    rows = per_step * blk
    prev = lambda n: jnp.maximum(n * per_step - 1, 0)
    blocks = [((rows, dq), BF16)] * 2 + [((blk, dkv), BF16)] * 2 + [((rows, dkv), BF16)] * 2
    bias = ((per_step, ATT_HEADS, blk, blk), F32)
    return pl.pallas_call(
        _swa_kernel,
        grid=(s // rows,),
        in_specs=[pl.BlockSpec(memory_space=pltpu.SMEM),
                  pl.BlockSpec(memory_space=pltpu.SMEM),
                  pl.BlockSpec((rows, dq), lambda n: (n, 0)),
                  pl.BlockSpec((blk, dkv), lambda n: (prev(n), k_blk)),
                  pl.BlockSpec((rows, dkv), lambda n: (n, k_blk)),
                  pl.BlockSpec((blk, dkv), lambda n: (prev(n), v_blk)),
                  pl.BlockSpec((rows, dkv), lambda n: (n, v_blk))],
        out_specs=pl.BlockSpec((rows, dq), lambda n: (n, 0)),
        out_shape=jax.ShapeDtypeStruct((s, dq), BF16),
        scratch_shapes=[pltpu.VMEM(*bias)],
        compiler_params=_params(1, _vmem_limit(blocks, [bias])),
        name="swa",
    )(slopes, sinks.astype(F32), qkv, qkv, qkv, qkv, qkv)


def kernel(x, norm_mix_g, ret_w_in, ret_w_out, att_w_qkv, att_b_qkv, att_sinks, att_w_out,
           norm_ffn_g, ffn_w_up, ffn_conv_w, ffn_conv_b, ffn_w_down, final_norm_g):
    batch, seq, d = x.shape
    assert batch == 1 and d == D_MODEL
    depth = norm_mix_g.shape[0]

    h = x.reshape(seq, d)
    hn = _rmsnorm(h, norm_mix_g[0])
    for i in range(depth):
        j = i // N_MIXERS
        if i % N_MIXERS == 0:
            decay_table, chunk_decay = _ret_decay_tables()
            proj, w_out = _ret_in_proj(hn, ret_w_in, j, decay_table, ret_w_out)
            mixed = _retention(proj, chunk_decay)
        else:
            qkv, w_out = _matmul_bias(hn, att_w_qkv, att_b_qkv, j, att_w_out, tm=1024, tn=1280)
            mixed = _swa(qkv, att_sinks[j])
        h, hn = _proj_res_norm(mixed, w_out, h, norm_ffn_g[i], emit_residual=True)

        gated, w_down = _ffn_up(hn, ffn_w_up, ffn_w_down, i, ffn_conv_w[i], ffn_conv_b[i])
        if i + 1 < depth:
            h, hn = _proj_res_norm(gated, w_down, h, norm_mix_g[i + 1], emit_residual=True)
        else:
            out = _proj_res_norm(gated, w_down, h, final_norm_g, emit_residual=False)
    return out.reshape(batch, seq, d)
```

```python
import functools

import jax
import jax.numpy as jnp
import numpy as np
from jax import lax
from jax.experimental import pallas as pl
from jax.experimental.pallas import tpu as pltpu

F32 = jnp.float32
BF16 = jnp.bfloat16

D_MODEL = 2048
N_MIXERS = 2

RET_HEADS = 8
RET_QK_DIM = D_MODEL // RET_HEADS
RET_V_DIM = 2 * D_MODEL // RET_HEADS
RET_CHUNK = 256

ATT_HEAD_DIM = 64
ATT_HEADS = D_MODEL // ATT_HEAD_DIM
ATT_KV_HEADS = ATT_HEADS // 8
ATT_GROUP = ATT_HEADS // ATT_KV_HEADS
WINDOW = 128
ATT_BLOCK = 128

D_FF = ((8 * D_MODEL // 3 + 127) // 128) * 128
CONV_WIDTH = 3
EPS = 1e-6

V7X_LANES = 128
V7X_SUBLANES = 8
V7X_VMEM_BYTES = 64 * 1024 * 1024
V7X_COMPILER_SCRATCH_BYTES = 8 * 1024 * 1024
V7X_VMEM_MIN_RESERVE_BYTES = 52 * 1024 * 1024

FF_TILE = 512
D_FF_PAD = ((D_FF + FF_TILE - 1) // FF_TILE) * FF_TILE

D_MODEL_PITCH = D_MODEL + V7X_LANES


def _nbytes(shape, dtype):
    n = 1
    for s in shape:
        n *= s
    return n * jnp.dtype(dtype).itemsize


def _vmem_limit(pipelined, resident=()):
    total = 2 * sum(_nbytes(s, d) for s, d in pipelined)
    total += sum(_nbytes(s, d) for s, d in resident)
    total += V7X_COMPILER_SCRATCH_BYTES
    assert total <= V7X_VMEM_BYTES, total
    return max(total, V7X_VMEM_MIN_RESERVE_BYTES)


def _params(n_grid, vmem_limit):
    return pltpu.CompilerParams(
        dimension_semantics=("arbitrary",) * n_grid, vmem_limit_bytes=vmem_limit)


def _store_padded(ref, rows, value):
    n, d = value.shape
    ref[rows, :d] = value.astype(ref.dtype)
    ref[rows, d:] = jnp.zeros((n, ref.shape[1] - d), ref.dtype)


def _gain_operand(gain, layer):
    d = gain.shape[-1]
    if layer is None:
        return gain.reshape(1, d), pl.BlockSpec((1, d), lambda *_: (0, 0))
    return gain.reshape(gain.shape[0], 1, d), pl.BlockSpec((None, 1, d), lambda *_: (layer, 0, 0))


def _rmsnorm_kernel(x_ref, g_ref, o_ref):
    x = x_ref[...]
    ms = jnp.mean(x * x, axis=-1, keepdims=True)
    _store_padded(o_ref, slice(None), x * lax.rsqrt(ms + EPS) * g_ref[...])


def _rmsnorm(x, gain, layer, *, tm=1024):
    m, d = x.shape
    gain, gain_spec = _gain_operand(gain, layer)
    blocks = [((tm, d), F32), ((1, d), F32), ((tm, D_MODEL_PITCH), BF16)]
    return pl.pallas_call(
        _rmsnorm_kernel,
        grid=(m // tm,),
        in_specs=[pl.BlockSpec((tm, d), lambda i: (i, 0)),
                  gain_spec],
        out_specs=pl.BlockSpec((tm, D_MODEL_PITCH), lambda i: (i, 0)),
        out_shape=jax.ShapeDtypeStruct((m, D_MODEL_PITCH), BF16),
        compiler_params=_params(1, _vmem_limit(blocks)),
        name="rmsnorm",
    )(x, gain)


def _matmul_bias_kernel(x_ref, w_ref, b_ref, side_ref, o_ref, side_bf_ref, wbf_ref):
    side_bf_ref[...] = side_ref[...].astype(BF16)

    @pl.when(pl.program_id(1) == 0)
    def _():
        wbf_ref[...] = w_ref[...].astype(BF16)

    k = wbf_ref.shape[0]
    acc = jnp.dot(x_ref[:, :k], wbf_ref[...], preferred_element_type=F32) + b_ref[...]
    o_ref[...] = acc.astype(o_ref.dtype)


def _matmul_bias(x, w, bias, layer, side_w, *, tm, tn):
    m, pitch = x.shape
    k, n = w.shape[1:]
    grid = (n // tn, m // tm)
    side_in, side_out, side_shape, side_blocks = _side_cast_specs(side_w, layer, grid)
    blocks = [((tm, pitch), BF16), ((k, tn), F32), ((1, tn), F32), ((tm, tn), BF16)] + side_blocks
    return pl.pallas_call(
        _matmul_bias_kernel,
        grid=grid,
        in_specs=[pl.BlockSpec((tm, pitch), lambda j, i: (i, 0)),
                  pl.BlockSpec((None, k, tn), lambda j, i: (layer, 0, j)),
                  pl.BlockSpec((None, 1, tn), lambda j, i: (layer, 0, j)),
                  side_in],
        out_specs=(pl.BlockSpec((tm, tn), lambda j, i: (i, j)), side_out),
        out_shape=(jax.ShapeDtypeStruct((m, n), BF16), side_shape),
        scratch_shapes=[pltpu.VMEM((k, tn), BF16)],
        compiler_params=_params(2, _vmem_limit(blocks, [((k, tn), BF16)])),
        name="matmul_bias",
    )(x, w, bias.reshape(bias.shape[0], 1, n), side_w)


def _ret_decay_tables():
    hh, dk, cs = RET_HEADS, RET_QK_DIM, RET_CHUNK
    log_gamma = np.log1p(-(2.0 ** (-5.0 - np.arange(hh, dtype=np.float32)))).astype(np.float32)
    n = np.arange(cs, dtype=np.float32)[:, None]
    q_scale = np.exp(log_gamma[None, :] * (n + 1.0 - cs))
    k_scale = np.exp(log_gamma[None, :] * (cs - 1.0 - n)) * np.float32(dk ** -0.5)
    table = np.concatenate([np.repeat(q_scale, dk, axis=1), np.repeat(k_scale, dk, axis=1)], axis=1)
    chunk_decay = np.exp(log_gamma * np.float32(cs))
    return jnp.asarray(table, F32), jnp.asarray(chunk_decay, F32)


RET_PROJ_SUB_ROWS = 512


def _ret_in_proj_kernel(x_ref, w_ref, tab_ref, side_ref, o_ref, side_bf_ref, wbf_ref, *, n_scaled, n_plain):
    side_bf_ref[...] = side_ref[...].astype(BF16)
    j = pl.program_id(0)
    tm = x_ref.shape[0]
    cs = tab_ref.shape[0]

    @pl.when(pl.program_id(1) == 0)
    def _():
        wbf_ref[...] = w_ref[...].astype(BF16)

    def run(epilogue):
        w = wbf_ref[...]
        for r0 in range(0, tm, RET_PROJ_SUB_ROWS):
            acc = jnp.dot(x_ref[r0:r0 + RET_PROJ_SUB_ROWS, :w.shape[0]], w, preferred_element_type=F32)
            o_ref[r0:r0 + RET_PROJ_SUB_ROWS, :] = epilogue(acc, r0).astype(o_ref.dtype)

    def scaled(acc, r0):
        del r0
        rows, cols = acc.shape
        return (acc.reshape(rows // cs, cs, cols) * tab_ref[...][None]).reshape(rows, cols)

    @pl.when(j < n_scaled)
    def _():
        run(scaled)

    @pl.when((j >= n_scaled) & (j < n_scaled + n_plain))
    def _():
        run(lambda acc, r0: acc)

    @pl.when(j >= n_scaled + n_plain)
    def _():
        run(lambda acc, r0: acc / (1.0 + jnp.exp(-acc)))


def _ret_in_proj(x, w, layer, table, side_w, *, tm=2048, tn=1024):
    m, pitch = x.shape
    k, n = w.shape[1:]
    cs, scaled_cols = table.shape
    assert scaled_cols % tn == 0 and RET_PROJ_SUB_ROWS % cs == 0 and tm % RET_PROJ_SUB_ROWS == 0
    n_scaled = scaled_cols // tn
    n_plain = RET_HEADS * RET_V_DIM // tn
    grid = (n // tn, m // tm)
    side_in, side_out, side_shape, side_blocks = _side_cast_specs(side_w, layer, grid)
    blocks = [((tm, pitch), BF16), ((k, tn), F32), ((cs, tn), F32), ((tm, tn), BF16)] + side_blocks
    return pl.pallas_call(
        functools.partial(_ret_in_proj_kernel, n_scaled=n_scaled, n_plain=n_plain),
        grid=grid,
        in_specs=[pl.BlockSpec((tm, pitch), lambda j, i: (i, 0)),
                  pl.BlockSpec((None, k, tn), lambda j, i: (layer, 0, j)),
                  pl.BlockSpec((cs, tn), lambda j, i: (0, jnp.minimum(j, n_scaled - 1))),
                  side_in],
        out_specs=(pl.BlockSpec((tm, tn), lambda j, i: (i, j)), side_out),
        out_shape=(jax.ShapeDtypeStruct((m, n), BF16), side_shape),
        scratch_shapes=[pltpu.VMEM((k, tn), BF16)],
        compiler_params=_params(2, _vmem_limit(blocks, [((k, tn), BF16)])),
        name="ret_in_proj",
    )(x, w, table, side_w)


RET_CHUNKS_PER_STEP = 2


SIDE_CAST_ROWS = 128


def _side_cast_specs(w, layer, grid):
    k, d = w.shape[1:]
    n_chunks = k // SIDE_CAST_ROWS
    n_steps = 1
    for g in grid:
        n_steps *= g
    assert k % SIDE_CAST_ROWS == 0 and n_chunks <= n_steps

    def chunk_of(*idx):
        step = 0
        for g, i in zip(grid, idx):
            step = step * g + i
        return jnp.minimum(step, n_chunks - 1)

    in_spec = pl.BlockSpec((None, SIDE_CAST_ROWS, d), lambda *idx: (layer, chunk_of(*idx), 0))
    out_spec = pl.BlockSpec((SIDE_CAST_ROWS, d), lambda *idx: (chunk_of(*idx), 0))
    blocks = [((SIDE_CAST_ROWS, d), F32), ((SIDE_CAST_ROWS, d), BF16)]
    return in_spec, out_spec, jax.ShapeDtypeStruct((k, d), BF16), blocks


def _retention_kernel(cd_ref, q_ref, k_ref, v_ref, g_ref, y_ref, state_ref, sbf_ref):
    hh, dk, dv, cs = RET_HEADS, RET_QK_DIM, RET_V_DIM, RET_CHUNK

    @pl.when(pl.program_id(0) == 0)
    def _():
        state_ref[...] = jnp.zeros_like(state_ref)
        sbf_ref[...] = jnp.zeros_like(sbf_ref)

    causal = (lax.broadcasted_iota(jnp.int32, (cs, cs), 0)
              >= lax.broadcasted_iota(jnp.int32, (cs, cs), 1))
    for r0 in range(0, q_ref.shape[0], cs):
        rows = slice(r0, r0 + cs)
        for h in range(hh):
            q = q_ref[rows, h * dk:(h + 1) * dk]
            k = k_ref[rows, h * dk:(h + 1) * dk]
            v = v_ref[rows, h * dv:(h + 1) * dv]
            s = lax.dot_general(q, k, (((1,), (1,)), ((), ())), preferred_element_type=F32)
            p = jnp.where(causal, s, 0.0).astype(BF16)
            o = (jnp.dot(p, v, preferred_element_type=F32)
                 + jnp.dot(q, sbf_ref[h], preferred_element_type=F32))
            upd = lax.dot_general(k, v, (((0,), (0,)), ((), ())), preferred_element_type=F32)
            new_state = (state_ref[h] + upd) * cd_ref[h]
            state_ref[h] = new_state
            sbf_ref[h] = new_state.astype(BF16)

            ms = jnp.mean(o * o, axis=-1, keepdims=True)
            gate = g_ref[rows, h * dv:(h + 1) * dv].astype(F32)
            y_ref[rows, h * dv:(h + 1) * dv] = (gate * (o * lax.rsqrt(ms + EPS))).astype(y_ref.dtype)


def _retention(proj, chunk_decay):
    s = proj.shape[0]
    hh, dk, dv, cs = RET_HEADS, RET_QK_DIM, RET_V_DIM, RET_CHUNK
    qk_w, v_w = hh * dk, hh * dv
    assert 2 * qk_w == v_w
    rows = RET_CHUNKS_PER_STEP * cs
    blocks = [((rows, qk_w), BF16)] * 2 + [((rows, v_w), BF16)] * 3
    scratch = [((hh, dk, dv), F32), ((hh, dk, dv), BF16)]
    return pl.pallas_call(
        _retention_kernel,
        grid=(s // rows,),
        in_specs=[pl.BlockSpec(memory_space=pltpu.SMEM),
                  pl.BlockSpec((rows, qk_w), lambda c: (c, 0)),
                  pl.BlockSpec((rows, qk_w), lambda c: (c, 1)),
                  pl.BlockSpec((rows, v_w), lambda c: (c, 1)),
                  pl.BlockSpec((rows, v_w), lambda c: (c, 2))],
        out_specs=pl.BlockSpec((rows, v_w), lambda c: (c, 0)),
        out_shape=jax.ShapeDtypeStruct((s, v_w), BF16),
        scratch_shapes=[pltpu.VMEM((hh, dk, dv), F32), pltpu.VMEM((hh, dk, dv), BF16)],
        compiler_params=_params(1, _vmem_limit(blocks, scratch)),
        name="retention",
    )(chunk_decay, proj, proj, proj, proj)


def _proj_res_norm_kernel(a_ref, w_ref, res_ref, gain_ref, *out_refs, emit_residual):
    h = res_ref[...] + jnp.dot(a_ref[...], w_ref[...], preferred_element_type=F32)
    ms = jnp.mean(h * h, axis=-1, keepdims=True)
    hn = h * lax.rsqrt(ms + EPS) * gain_ref[...]
    if emit_residual:
        h_ref, hn_ref = out_refs
        h_ref[...] = h
        _store_padded(hn_ref, slice(None), hn)
    else:
        (hn_ref,) = out_refs
        hn_ref[...] = hn.astype(hn_ref.dtype)


def _proj_res_norm(a, w, res, gain, layer, *, emit_residual, tm=512):
    gain, gain_spec = _gain_operand(gain, layer)
    m = a.shape[0]
    k, d = w.shape
    row = lambda s: (s, 0)
    if emit_residual:
        out_shape = (jax.ShapeDtypeStruct((m, d), F32), jax.ShapeDtypeStruct((m, D_MODEL_PITCH), BF16))
        out_specs = (pl.BlockSpec((tm, d), row), pl.BlockSpec((tm, D_MODEL_PITCH), row))
        out_blocks = [((tm, d), F32), ((tm, D_MODEL_PITCH), BF16)]
    else:
        out_shape = jax.ShapeDtypeStruct((m, d), F32)
        out_specs = pl.BlockSpec((tm, d), row)
        out_blocks = [((tm, d), F32)]
    blocks = [((tm, k), BF16), ((tm, d), F32), ((1, d), F32)] + out_blocks
    return pl.pallas_call(
        functools.partial(_proj_res_norm_kernel, emit_residual=emit_residual),
        grid=(m // tm,),
        in_specs=[pl.BlockSpec((tm, k), row),
                  pl.BlockSpec((k, d), lambda s: (0, 0), pipeline_mode=pl.Buffered(1)),
                  pl.BlockSpec((tm, d), row),
                  gain_spec],
        out_specs=out_specs,
        out_shape=out_shape,
        compiler_params=_params(1, _vmem_limit(blocks, [((k, d), BF16)])),
        name="proj_res_norm" if emit_residual else "proj_res_final_norm",
    )(a, w, res, gain)


FF_FULL_TILES = D_FF // FF_TILE
FF_REM = D_FF % FF_TILE
FF_NUM_TILES = FF_FULL_TILES + 1
FF_EPI_ROWS = 32


FF_SUB_ROWS = 256


def _ffn_up_kernel(x_ref, wa_ref, wb0_ref, wb1_ref, cw_ref, cb_ref, wd_ref, g_ref, wd_bf_ref, wbf_ref,
                   *stage_refs):
    wd_bf_ref[...] = wd_ref[...].astype(BF16)
    j = pl.program_id(0)
    i = pl.program_id(1)
    tm = x_ref.shape[0]
    tf = wa_ref.shape[1]
    head = wb0_ref.shape[1]
    hist = V7X_SUBLANES
    sub_rows = FF_SUB_ROWS
    n_sub = tm // sub_rows
    assert n_sub % 2 == 0
    slots = (stage_refs[0:3], stage_refs[3:6])
    last_u1, last_u2 = slots[1][1], slots[1][2]

    @pl.when((i == 0) & (j == 0))
    def _():
        for _, u1_ref, u2_ref in slots:
            u1_ref[sub_rows:, :] = jnp.zeros((hist, 2 * tf), F32)
            u2_ref[sub_rows:, :] = jnp.zeros((hist, 2 * tf), F32)

    @pl.when(i == 0)
    def _():
        wbf_ref[:, :tf] = wa_ref[...].astype(BF16)
        wbf_ref[:, tf:tf + head] = wb0_ref[...].astype(BF16)
        last_u1[sub_rows:, :] = jnp.zeros((hist, 2 * tf), F32)
        last_u2[sub_rows:, :] = jnp.zeros((hist, 2 * tf), F32)

    @pl.when((i == 0) & (j < FF_FULL_TILES))
    def _():
        wbf_ref[:, tf + head:] = wb1_ref[:, :FF_REM].astype(BF16)

    @pl.when((i == 0) & (j == FF_FULL_TILES))
    def _():
        valid = FF_REM - head
        wbf_ref[:, tf + head:tf + head + valid] = wb1_ref[:, :valid].astype(BF16)
        wbf_ref[:, tf + head + valid:] = jnp.zeros((wbf_ref.shape[0], tf - head - valid), BF16)

    def conv(slot, r0, stage_c0, param_c0):
        u0_ref, u1_ref, u2_ref = slot
        rows = slice(r0, r0 + FF_EPI_ROWS)
        cols = slice(stage_c0, stage_c0 + V7X_LANES)
        pcols = slice(param_c0, param_c0 + V7X_LANES)
        sub = V7X_SUBLANES
        tap = [cw_ref[t * sub:(t + 1) * sub, pcols][None] for t in range(CONV_WIDTH)]
        piece = lambda ref: ref[rows, cols].reshape(FF_EPI_ROWS // sub, sub, V7X_LANES)
        c = cb_ref[:, pcols][None] + (tap[0] * piece(u2_ref) + tap[1] * piece(u1_ref) + tap[2] * piece(u0_ref))
        return c.reshape(FF_EPI_ROWS, V7X_LANES)

    def run(width):
        w = wbf_ref[...]
        if width < tf:
            w = jnp.concatenate([w[:, :width], w[:, tf:tf + width]], axis=1)
        for k in range(n_sub):
            s0 = k * sub_rows
            slot = slots[k % 2]
            u0_ref, u1_ref, u2_ref = slot
            _, prev_u1, prev_u2 = slots[(k + 1) % 2]
            u = jnp.dot(x_ref[s0:s0 + sub_rows, :w.shape[0]], w, preferred_element_type=F32)
            live = slice(0, 2 * width)
            u1_ref[0:hist, live] = prev_u1[sub_rows:, live]
            u2_ref[0:hist, live] = prev_u2[sub_rows:, live]
            u0_ref[:, live] = u
            u1_ref[1:1 + sub_rows, live] = u
            u2_ref[2:2 + sub_rows, live] = u
            for r0 in range(0, sub_rows, FF_EPI_ROWS):
                for c0 in range(0, width, V7X_LANES):
                    a = conv(slot, r0, c0, c0)
                    b = conv(slot, r0, width + c0, tf + c0)
                    g_ref[s0 + r0:s0 + r0 + FF_EPI_ROWS, c0:c0 + V7X_LANES] = (
                        a / (1.0 + jnp.exp(-a)) * b).astype(g_ref.dtype)
            if width < tf:
                g_ref[s0:s0 + sub_rows, width:] = jnp.zeros((sub_rows, tf - width), g_ref.dtype)

    @pl.when(j < FF_FULL_TILES)
    def _():
        run(tf)

    @pl.when(j == FF_FULL_TILES)
    def _():
        run(FF_REM)


def _ffn_conv_tables(conv_w, conv_b, *, tf=FF_TILE):
    nf, sub = FF_NUM_TILES, V7X_SUBLANES
    def per_tile(t):
        layers, taps = t.shape[:2]
        t = jnp.pad(t.reshape(layers, taps, 2, D_FF), ((0, 0), (0, 0), (0, 0), (0, D_FF_PAD - D_FF)))
        t = t.reshape(layers, taps, 2, nf, tf).transpose(0, 3, 1, 2, 4).reshape(layers, nf, taps, 1, 2 * tf)
        return jnp.broadcast_to(t, (layers, nf, taps, sub, 2 * tf)).reshape(layers, nf, taps * sub, 2 * tf)
    return per_tile(conv_w), per_tile(conv_b[:, None, :])


def _ffn_up(x, w_up, w_down, layer, cw, cb, *, tm=2048, tf=FF_TILE):
    m, pitch = x.shape
    d = w_up.shape[1]
    nf = FF_NUM_TILES
    head = tf - FF_REM
    assert tf % head == 0 and FF_REM % head == 0
    sub = V7X_SUBLANES
    grid = (nf, m // tm)
    wd_in, wd_out, wd_shape, wd_blocks = _side_cast_specs(w_down, layer, grid)
    blocks = ([((tm, pitch), BF16)] + [((d, tf), F32)] * 2 + [((d, head), F32)]
              + [((CONV_WIDTH * sub, 2 * tf), F32)] + [((sub, 2 * tf), F32)] + [((tm, tf), BF16)]
              + wd_blocks)
    slot = [((FF_SUB_ROWS, 2 * tf), F32)] + [((FF_SUB_ROWS + V7X_SUBLANES, 2 * tf), F32)] * 2
    scratch = [((d, 2 * tf), BF16)] + slot * 2
    return pl.pallas_call(
        _ffn_up_kernel,
        grid=grid,
        in_specs=[pl.BlockSpec((tm, pitch), lambda j, i: (i, 0)),
                  pl.BlockSpec((None, d, tf), lambda j, i: (layer, 0, j)),
                  pl.BlockSpec((None, d, head),
                               lambda j, i: (layer, 0, (FF_FULL_TILES + j) * (tf // head) + FF_REM // head)),
                  pl.BlockSpec((None, d, tf), lambda j, i: (layer, 0, FF_FULL_TILES + 1 + j)),
                  pl.BlockSpec((None, None, CONV_WIDTH * sub, 2 * tf), lambda j, i: (layer, j, 0, 0)),
                  pl.BlockSpec((None, None, sub, 2 * tf), lambda j, i: (layer, j, 0, 0)),
                  wd_in],
        out_specs=(pl.BlockSpec((tm, tf), lambda j, i: (i, j)), wd_out),
        out_shape=(jax.ShapeDtypeStruct((m, D_FF_PAD), BF16), wd_shape),
        scratch_shapes=[pltpu.VMEM(*shape_dtype) for shape_dtype in scratch],
        compiler_params=_params(2, _vmem_limit(blocks, scratch)),
        name="ffn_up",
    )(x, w_up, w_up, w_up, cw, cb, w_down)


SWA_BLOCKS_PER_STEP = 1


def _swa_kernel(slopes_ref, sinks_ref, q_ref, kp_ref, kc_ref, vp_ref, vc_ref, o_ref, bias_ref):
    n = pl.program_id(0)
    blk, dh, grp = ATT_BLOCK, ATT_HEAD_DIM, ATT_GROUP
    lanes = 2 * dh
    pairs = grp // 2

    qi = lax.broadcasted_iota(jnp.int32, (blk, blk), 0)
    kj = lax.broadcasted_iota(jnp.int32, (blk, blk), 1)
    from_prev = kj > qi

    def fill_bias(table, first_block):
        dist_f = jnp.where(from_prev, blk + qi - kj, qi - kj).astype(F32)
        assert WINDOW == blk
        for head in range(ATT_HEADS):
            bias = -(slopes_ref[head] * dist_f)
            bias_ref[table, head] = jnp.where(from_prev, -jnp.inf, bias) if first_block else bias

    @pl.when(n == 0)
    def _():
        fill_bias(0, True)
        for table in range(1, SWA_BLOCKS_PER_STEP):
            fill_bias(table, False)

    @pl.when(n == 1)
    def _():
        fill_bias(0, False)

    lane_kv = lax.broadcasted_iota(jnp.int32, (2 * blk, lanes), 1)
    lane_o = lax.broadcasted_iota(jnp.int32, (blk, lanes), 1)

    for b in range(SWA_BLOCKS_PER_STEP):
        rows = slice(b * blk, (b + 1) * blk)
        prev_rows = slice((b - 1) * blk, b * blk)
        for pair in range(ATT_KV_HEADS // 2):
            cols = slice(pair * lanes, (pair + 1) * lanes)
            k_prev = kp_ref[:, cols] if b == 0 else kc_ref[prev_rows, cols]
            v_prev = vp_ref[:, cols] if b == 0 else vc_ref[prev_rows, cols]
            k2 = jnp.concatenate([k_prev, kc_ref[rows, cols]], axis=0).astype(F32) * (dh ** -0.5)
            v2 = jnp.concatenate([v_prev, vc_ref[rows, cols]], axis=0).astype(F32)
            k2s = pltpu.roll(k2, dh, axis=1)
            v2s = pltpu.roll(v2, dh, axis=1)
            for sub in range(2):
                kvh = 2 * pair + sub
                k_lo, k_hi = (k2, k2s) if sub == 0 else (k2s, k2)
                v_lo, v_hi = (v2, v2s) if sub == 0 else (v2s, v2)
                kz = (jnp.where(lane_kv < dh, k_lo, 0.0).astype(BF16),
                      jnp.where(lane_kv >= dh, k_hi, 0.0).astype(BF16))
                vz = (jnp.where(lane_kv < dh, v_lo, 0.0).astype(BF16),
                      jnp.where(lane_kv >= dh, v_hi, 0.0).astype(BF16))
                qcol0 = kvh * grp * dh
                q_stack = jnp.concatenate(
                    [q_ref[rows, qcol0 + hp * lanes:qcol0 + (hp + 1) * lanes] for hp in range(pairs)], axis=0)
                acc = None
                inv = [[None, None] for _ in range(pairs)]
                for par in range(2):
                    s_all = lax.dot_general(q_stack, kz[par], (((1,), (1,)), ((), ())),
                                            preferred_element_type=F32)
                    e_parts = []
                    for hp in range(pairs):
                        head = kvh * grp + 2 * hp + par
                        s2 = s_all[hp * blk:(hp + 1) * blk]
                        s = jnp.where(from_prev, s2[:, :blk], s2[:, blk:]) + bias_ref[b, head]
                        sink = sinks_ref[head]
                        m = jnp.maximum(jnp.max(s, axis=-1, keepdims=True), sink)
                        e = jnp.exp(s - m)
                        denom = jnp.sum(e, axis=-1, keepdims=True) + jnp.exp(sink - m)
                        inv[hp][par] = 1.0 / denom
                        e = e.astype(BF16)
                        zero = jnp.zeros_like(e)
                        e_parts.append(jnp.concatenate(
                            [jnp.where(from_prev, e, zero), jnp.where(from_prev, zero, e)], axis=1))
                    pv = jnp.dot(jnp.concatenate(e_parts, axis=0), vz[par], preferred_element_type=F32)
                    acc = pv if acc is None else acc + pv
                for hp in range(pairs):
                    scale = jnp.where(lane_o < dh, inv[hp][0], inv[hp][1])
                    o_ref[rows, qcol0 + hp * lanes:qcol0 + (hp + 1) * lanes] = (
                        acc[hp * blk:(hp + 1) * blk] * scale).astype(o_ref.dtype)


def _swa(qkv, sinks):
    s = qkv.shape[0]
    blk = ATT_BLOCK
    dq = ATT_HEADS * ATT_HEAD_DIM
    dkv = ATT_KV_HEADS * ATT_HEAD_DIM
    k_blk = dq // dkv
    v_blk = k_blk + 1
    slopes = 2.0 ** (-8.0 * jnp.arange(1, ATT_HEADS + 1, dtype=F32) / ATT_HEADS)
    per_step = SWA_BLOCKS_PER_STEP
    rows = per_step * blk
    prev = lambda n: jnp.maximum(n * per_step - 1, 0)
    blocks = [((rows, dq), BF16)] * 2 + [((blk, dkv), BF16)] * 2 + [((rows, dkv), BF16)] * 2
    bias = ((per_step, ATT_HEADS, blk, blk), F32)
    return pl.pallas_call(
        _swa_kernel,
        grid=(s // rows,),
        in_specs=[pl.BlockSpec(memory_space=pltpu.SMEM),
                  pl.BlockSpec(memory_space=pltpu.SMEM),
                  pl.BlockSpec((rows, dq), lambda n: (n, 0)),
                  pl.BlockSpec((blk, dkv), lambda n: (prev(n), k_blk)),
                  pl.BlockSpec((rows, dkv), lambda n: (n, k_blk)),
                  pl.BlockSpec((blk, dkv), lambda n: (prev(n), v_blk)),
                  pl.BlockSpec((rows, dkv), lambda n: (n, v_blk))],
        out_specs=pl.BlockSpec((rows, dq), lambda n: (n, 0)),
        out_shape=jax.ShapeDtypeStruct((s, dq), BF16),
        scratch_shapes=[pltpu.VMEM(*bias)],
        compiler_params=_params(1, _vmem_limit(blocks, [bias])),
        name="swa",
    )(slopes, sinks.astype(F32), qkv, qkv, qkv, qkv, qkv)


def kernel(x, norm_mix_g, ret_w_in, ret_w_out, att_w_qkv, att_b_qkv, att_sinks, att_w_out,
           norm_ffn_g, ffn_w_up, ffn_conv_w, ffn_conv_b, ffn_w_down, final_norm_g):
    batch, seq, d = x.shape
    assert batch == 1 and d == D_MODEL
    depth = norm_mix_g.shape[0]

    h = x.reshape(seq, d)
    hn = _rmsnorm(h, norm_mix_g, 0)
    conv_taps, conv_bias = _ffn_conv_tables(ffn_conv_w, ffn_conv_b)
    for i in range(depth):
        j = i // N_MIXERS
        if i % N_MIXERS == 0:
            decay_table, chunk_decay = _ret_decay_tables()
            proj, w_out = _ret_in_proj(hn, ret_w_in, j, decay_table, ret_w_out)
            mixed = _retention(proj, chunk_decay)
        else:
            qkv, w_out = _matmul_bias(hn, att_w_qkv, att_b_qkv, j, att_w_out, tm=1024, tn=1280)
            mixed = _swa(qkv, att_sinks[j])
        h, hn = _proj_res_norm(mixed, w_out, h, norm_ffn_g, i, emit_residual=True)

        gated, w_down = _ffn_up(hn, ffn_w_up, ffn_w_down, i, conv_taps, conv_bias)
        if i + 1 < depth:
            h, hn = _proj_res_norm(gated, w_down, h, norm_mix_g, i + 1, emit_residual=True)
        else:
            out = _proj_res_norm(gated, w_down, h, final_norm_g, None, emit_residual=False)
    return out.reshape(batch, seq, d)
```

```python
import functools

import jax
import jax.numpy as jnp
import numpy as np
from jax import lax
from jax.experimental import pallas as pl
from jax.experimental.pallas import tpu as pltpu

F32 = jnp.float32
BF16 = jnp.bfloat16

D_MODEL = 2048
N_MIXERS = 2

RET_HEADS = 8
RET_QK_DIM = D_MODEL // RET_HEADS
RET_V_DIM = 2 * D_MODEL // RET_HEADS
RET_CHUNK = 256

ATT_HEAD_DIM = 64
ATT_HEADS = D_MODEL // ATT_HEAD_DIM
ATT_KV_HEADS = ATT_HEADS // 8
ATT_GROUP = ATT_HEADS // ATT_KV_HEADS
WINDOW = 128
ATT_BLOCK = 128

D_FF = ((8 * D_MODEL // 3 + 127) // 128) * 128
CONV_WIDTH = 3
EPS = 1e-6

V7X_LANES = 128
V7X_SUBLANES = 8
V7X_VMEM_BYTES = 64 * 1024 * 1024
V7X_COMPILER_SCRATCH_BYTES = 8 * 1024 * 1024
V7X_VMEM_MIN_RESERVE_BYTES = 52 * 1024 * 1024

FF_TILE = 512
D_FF_PAD = ((D_FF + FF_TILE - 1) // FF_TILE) * FF_TILE

D_MODEL_PITCH = D_MODEL + V7X_LANES


def _nbytes(shape, dtype):
    n = 1
    for s in shape:
        n *= s
    return n * jnp.dtype(dtype).itemsize


def _vmem_limit(pipelined, resident=()):
    total = 2 * sum(_nbytes(s, d) for s, d in pipelined)
    total += sum(_nbytes(s, d) for s, d in resident)
    total += V7X_COMPILER_SCRATCH_BYTES
    assert total <= V7X_VMEM_BYTES, total
    return max(total, V7X_VMEM_MIN_RESERVE_BYTES)


def _params(n_grid, vmem_limit):
    return pltpu.CompilerParams(
        dimension_semantics=("arbitrary",) * n_grid, vmem_limit_bytes=vmem_limit)


def _store_padded(ref, rows, value):
    n, d = value.shape
    ref[rows, :d] = value.astype(ref.dtype)
    ref[rows, d:] = jnp.zeros((n, ref.shape[1] - d), ref.dtype)


def _gain_operand(gain, layer):
    d = gain.shape[-1]
    if layer is None:
        return gain.reshape(1, d), pl.BlockSpec((1, d), lambda *_: (0, 0))
    return gain.reshape(gain.shape[0], 1, d), pl.BlockSpec((None, 1, d), lambda *_: (layer, 0, 0))


def _rmsnorm_kernel(x_ref, g_ref, o_ref):
    x = x_ref[...]
    ms = jnp.mean(x * x, axis=-1, keepdims=True)
    _store_padded(o_ref, slice(None), x * lax.rsqrt(ms + EPS) * g_ref[...])


def _rmsnorm(x, gain, layer, *, tm=1024):
    m, d = x.shape
    gain, gain_spec = _gain_operand(gain, layer)
    blocks = [((tm, d), F32), ((1, d), F32), ((tm, D_MODEL_PITCH), BF16)]
    return pl.pallas_call(
        _rmsnorm_kernel,
        grid=(m // tm,),
        in_specs=[pl.BlockSpec((tm, d), lambda i: (i, 0)),
                  gain_spec],
        out_specs=pl.BlockSpec((tm, D_MODEL_PITCH), lambda i: (i, 0)),
        out_shape=jax.ShapeDtypeStruct((m, D_MODEL_PITCH), BF16),
        compiler_params=_params(1, _vmem_limit(blocks)),
        name="rmsnorm",
    )(x, gain)


def _matmul_bias_kernel(x_ref, w_ref, b_ref, side_ref, o_ref, side_bf_ref, wbf_ref):
    side_bf_ref[...] = side_ref[...].astype(BF16)

    @pl.when(pl.program_id(1) == 0)
    def _():
        wbf_ref[...] = w_ref[...].astype(BF16)

    k = wbf_ref.shape[0]
    acc = jnp.dot(x_ref[:, :k], wbf_ref[...], preferred_element_type=F32) + b_ref[...]
    o_ref[...] = acc.astype(o_ref.dtype)


def _matmul_bias(x, w, bias, layer, side_w, *, tm, tn):
    m, pitch = x.shape
    k, n = w.shape[1:]
    grid = (n // tn, m // tm)
    side_in, side_out, side_shape, side_blocks = _side_cast_specs(side_w, layer, grid)
    blocks = [((tm, pitch), BF16), ((k, tn), F32), ((1, tn), F32), ((tm, tn), BF16)] + side_blocks
    return pl.pallas_call(
        _matmul_bias_kernel,
        grid=grid,
        in_specs=[pl.BlockSpec((tm, pitch), lambda j, i: (i, 0)),
                  pl.BlockSpec((None, k, tn), lambda j, i: (layer, 0, j)),
                  pl.BlockSpec((None, 1, tn), lambda j, i: (layer, 0, j)),
                  side_in],
        out_specs=(pl.BlockSpec((tm, tn), lambda j, i: (i, j)), side_out),
        out_shape=(jax.ShapeDtypeStruct((m, n), BF16), side_shape),
        scratch_shapes=[pltpu.VMEM((k, tn), BF16)],
        compiler_params=_params(2, _vmem_limit(blocks, [((k, tn), BF16)])),
        name="matmul_bias",
    )(x, w, bias.reshape(bias.shape[0], 1, n), side_w)


def _ret_decay_tables():
    hh, dk, cs = RET_HEADS, RET_QK_DIM, RET_CHUNK
    log_gamma = np.log1p(-(2.0 ** (-5.0 - np.arange(hh, dtype=np.float32)))).astype(np.float32)
    n = np.arange(cs, dtype=np.float32)[:, None]
    q_scale = np.exp(log_gamma[None, :] * (n + 1.0 - cs))
    k_scale = np.exp(log_gamma[None, :] * (cs - 1.0 - n)) * np.float32(dk ** -0.5)
    table = np.concatenate([np.repeat(q_scale, dk, axis=1), np.repeat(k_scale, dk, axis=1)], axis=1)
    chunk_decay = np.exp(log_gamma * np.float32(cs))
    return jnp.asarray(table, F32), jnp.asarray(chunk_decay, F32)


RET_PROJ_SUB_ROWS = 256


def _ret_in_proj_kernel(x_ref, w_ref, tab_ref, side_ref, o_ref, side_bf_ref, wbf_ref, *, n_scaled, n_plain):
    side_bf_ref[...] = side_ref[...].astype(BF16)
    j = pl.program_id(0)
    tm = x_ref.shape[0]
    cs = tab_ref.shape[0]

    @pl.when(pl.program_id(1) == 0)
    def _():
        wbf_ref[...] = w_ref[...].astype(BF16)

    def run(epilogue):
        w = wbf_ref[...]
        for r0 in range(0, tm, RET_PROJ_SUB_ROWS):
            acc = jnp.dot(x_ref[r0:r0 + RET_PROJ_SUB_ROWS, :w.shape[0]], w, preferred_element_type=F32)
            o_ref[r0:r0 + RET_PROJ_SUB_ROWS, :] = epilogue(acc, r0).astype(o_ref.dtype)

    def scaled(acc, r0):
        del r0
        rows, cols = acc.shape
        return (acc.reshape(rows // cs, cs, cols) * tab_ref[...][None]).reshape(rows, cols)

    @pl.when(j < n_scaled)
    def _():
        run(scaled)

    @pl.when((j >= n_scaled) & (j < n_scaled + n_plain))
    def _():
        run(lambda acc, r0: acc)

    @pl.when(j >= n_scaled + n_plain)
    def _():
        run(lambda acc, r0: acc / (1.0 + jnp.exp(-acc)))


def _ret_in_proj(x, w, layer, table, side_w, *, tm=2048, tn=1024):
    m, pitch = x.shape
    k, n = w.shape[1:]
    cs, scaled_cols = table.shape
    assert scaled_cols % tn == 0 and RET_PROJ_SUB_ROWS % cs == 0 and tm % RET_PROJ_SUB_ROWS == 0
    n_scaled = scaled_cols // tn
    n_plain = RET_HEADS * RET_V_DIM // tn
    grid = (n // tn, m // tm)
    side_in, side_out, side_shape, side_blocks = _side_cast_specs(side_w, layer, grid)
    blocks = [((tm, pitch), BF16), ((k, tn), F32), ((cs, tn), F32), ((tm, tn), BF16)] + side_blocks
    return pl.pallas_call(
        functools.partial(_ret_in_proj_kernel, n_scaled=n_scaled, n_plain=n_plain),
        grid=grid,
        in_specs=[pl.BlockSpec((tm, pitch), lambda j, i: (i, 0)),
                  pl.BlockSpec((None, k, tn), lambda j, i: (layer, 0, j)),
                  pl.BlockSpec((cs, tn), lambda j, i: (0, jnp.minimum(j, n_scaled - 1))),
                  side_in],
        out_specs=(pl.BlockSpec((tm, tn), lambda j, i: (i, j)), side_out),
        out_shape=(jax.ShapeDtypeStruct((m, n), BF16), side_shape),
        scratch_shapes=[pltpu.VMEM((k, tn), BF16)],
        compiler_params=_params(2, _vmem_limit(blocks, [((k, tn), BF16)])),
        name="ret_in_proj",
    )(x, w, table, side_w)


RET_CHUNKS_PER_STEP = 2


SIDE_CAST_ROWS = 128


def _side_cast_specs(w, layer, grid):
    k, d = w.shape[1:]
    n_chunks = k // SIDE_CAST_ROWS
    n_steps = 1
    for g in grid:
        n_steps *= g
    assert k % SIDE_CAST_ROWS == 0 and n_chunks <= n_steps

    def chunk_of(*idx):
        step = 0
        for g, i in zip(grid, idx):
            step = step * g + i
        return jnp.minimum(step, n_chunks - 1)

    in_spec = pl.BlockSpec((None, SIDE_CAST_ROWS, d), lambda *idx: (layer, chunk_of(*idx), 0))
    out_spec = pl.BlockSpec((SIDE_CAST_ROWS, d), lambda *idx: (chunk_of(*idx), 0))
    blocks = [((SIDE_CAST_ROWS, d), F32), ((SIDE_CAST_ROWS, d), BF16)]
    return in_spec, out_spec, jax.ShapeDtypeStruct((k, d), BF16), blocks


def _retention_kernel(cd_ref, q_ref, k_ref, v_ref, g_ref, y_ref, state_ref, sbf_ref):
    hh, dk, dv, cs = RET_HEADS, RET_QK_DIM, RET_V_DIM, RET_CHUNK

    @pl.when(pl.program_id(0) == 0)
    def _():
        state_ref[...] = jnp.zeros_like(state_ref)
        sbf_ref[...] = jnp.zeros_like(sbf_ref)

    causal = (lax.broadcasted_iota(jnp.int32, (cs, cs), 0)
              >= lax.broadcasted_iota(jnp.int32, (cs, cs), 1))
    for r0 in range(0, q_ref.shape[0], cs):
        rows = slice(r0, r0 + cs)
        for h in range(hh):
            q = q_ref[rows, h * dk:(h + 1) * dk]
            k = k_ref[rows, h * dk:(h + 1) * dk]
            v = v_ref[rows, h * dv:(h + 1) * dv]
            s = lax.dot_general(q, k, (((1,), (1,)), ((), ())), preferred_element_type=F32)
            p = jnp.where(causal, s, 0.0).astype(BF16)
            o = (jnp.dot(p, v, preferred_element_type=F32)
                 + jnp.dot(q, sbf_ref[h], preferred_element_type=F32))
            upd = lax.dot_general(k, v, (((0,), (0,)), ((), ())), preferred_element_type=F32)
            new_state = (state_ref[h] + upd) * cd_ref[h]
            state_ref[h] = new_state
            sbf_ref[h] = new_state.astype(BF16)

            ms = jnp.mean(o * o, axis=-1, keepdims=True)
            gate = g_ref[rows, h * dv:(h + 1) * dv].astype(F32)
            y_ref[rows, h * dv:(h + 1) * dv] = (gate * (o * lax.rsqrt(ms + EPS))).astype(y_ref.dtype)


def _retention(proj, chunk_decay):
    s = proj.shape[0]
    hh, dk, dv, cs = RET_HEADS, RET_QK_DIM, RET_V_DIM, RET_CHUNK
    qk_w, v_w = hh * dk, hh * dv
    assert 2 * qk_w == v_w
    rows = RET_CHUNKS_PER_STEP * cs
    blocks = [((rows, qk_w), BF16)] * 2 + [((rows, v_w), BF16)] * 3
    scratch = [((hh, dk, dv), F32), ((hh, dk, dv), BF16)]
    return pl.pallas_call(
        _retention_kernel,
        grid=(s // rows,),
        in_specs=[pl.BlockSpec(memory_space=pltpu.SMEM),
                  pl.BlockSpec((rows, qk_w), lambda c: (c, 0)),
                  pl.BlockSpec((rows, qk_w), lambda c: (c, 1)),
                  pl.BlockSpec((rows, v_w), lambda c: (c, 1)),
                  pl.BlockSpec((rows, v_w), lambda c: (c, 2))],
        out_specs=pl.BlockSpec((rows, v_w), lambda c: (c, 0)),
        out_shape=jax.ShapeDtypeStruct((s, v_w), BF16),
        scratch_shapes=[pltpu.VMEM((hh, dk, dv), F32), pltpu.VMEM((hh, dk, dv), BF16)],
        compiler_params=_params(1, _vmem_limit(blocks, scratch)),
        name="retention",
    )(chunk_decay, proj, proj, proj, proj)


def _proj_res_norm_kernel(a_ref, w_ref, res_ref, gain_ref, *out_refs, emit_residual):
    h = res_ref[...] + jnp.dot(a_ref[...], w_ref[...], preferred_element_type=F32)
    ms = jnp.mean(h * h, axis=-1, keepdims=True)
    hn = h * lax.rsqrt(ms + EPS) * gain_ref[...]
    if emit_residual:
        h_ref, hn_ref = out_refs
        h_ref[...] = h
        _store_padded(hn_ref, slice(None), hn)
    else:
        (hn_ref,) = out_refs
        hn_ref[...] = hn.astype(hn_ref.dtype)


def _proj_res_norm(a, w, res, gain, layer, *, emit_residual, tm=512):
    gain, gain_spec = _gain_operand(gain, layer)
    m = a.shape[0]
    k, d = w.shape
    row = lambda s: (s, 0)
    if emit_residual:
        out_shape = (jax.ShapeDtypeStruct((m, d), F32), jax.ShapeDtypeStruct((m, D_MODEL_PITCH), BF16))
        out_specs = (pl.BlockSpec((tm, d), row), pl.BlockSpec((tm, D_MODEL_PITCH), row))
        out_blocks = [((tm, d), F32), ((tm, D_MODEL_PITCH), BF16)]
    else:
        out_shape = jax.ShapeDtypeStruct((m, d), F32)
        out_specs = pl.BlockSpec((tm, d), row)
        out_blocks = [((tm, d), F32)]
    blocks = [((tm, k), BF16), ((tm, d), F32), ((1, d), F32)] + out_blocks
    return pl.pallas_call(
        functools.partial(_proj_res_norm_kernel, emit_residual=emit_residual),
        grid=(m // tm,),
        in_specs=[pl.BlockSpec((tm, k), row),
                  pl.BlockSpec((k, d), lambda s: (0, 0), pipeline_mode=pl.Buffered(1)),
                  pl.BlockSpec((tm, d), row),
                  gain_spec],
        out_specs=out_specs,
        out_shape=out_shape,
        compiler_params=_params(1, _vmem_limit(blocks, [((k, d), BF16)])),
        name="proj_res_norm" if emit_residual else "proj_res_final_norm",
    )(a, w, res, gain)


FF_FULL_TILES = D_FF // FF_TILE
FF_REM = D_FF % FF_TILE
FF_NUM_TILES = FF_FULL_TILES + 1
FF_EPI_ROWS = 32


FF_SUB_ROWS = 256


def _ffn_up_kernel(x_ref, wa_ref, wb0_ref, wb1_ref, cw_ref, cb_ref, wd_ref, g_ref, wd_bf_ref, wbf_ref,
                   *stage_refs):
    wd_bf_ref[...] = wd_ref[...].astype(BF16)
    j = pl.program_id(0)
    i = pl.program_id(1)
    tm = x_ref.shape[0]
    tf = wa_ref.shape[1]
    head = wb0_ref.shape[1]
    hist = V7X_SUBLANES
    sub_rows = FF_SUB_ROWS
    n_sub = tm // sub_rows
    assert n_sub % 2 == 0
    slots = (stage_refs[0:3], stage_refs[3:6])
    last_u1, last_u2 = slots[1][1], slots[1][2]

    @pl.when((i == 0) & (j == 0))
    def _():
        for _, u1_ref, u2_ref in slots:
            u1_ref[sub_rows:, :] = jnp.zeros((hist, 2 * tf), F32)
            u2_ref[sub_rows:, :] = jnp.zeros((hist, 2 * tf), F32)

    @pl.when(i == 0)
    def _():
        wbf_ref[:, :tf] = wa_ref[...].astype(BF16)
        wbf_ref[:, tf:tf + head] = wb0_ref[...].astype(BF16)
        last_u1[sub_rows:, :] = jnp.zeros((hist, 2 * tf), F32)
        last_u2[sub_rows:, :] = jnp.zeros((hist, 2 * tf), F32)

    @pl.when((i == 0) & (j < FF_FULL_TILES))
    def _():
        wbf_ref[:, tf + head:] = wb1_ref[:, :FF_REM].astype(BF16)

    @pl.when((i == 0) & (j == FF_FULL_TILES))
    def _():
        valid = FF_REM - head
        wbf_ref[:, tf + head:tf + head + valid] = wb1_ref[:, :valid].astype(BF16)
        wbf_ref[:, tf + head + valid:] = jnp.zeros((wbf_ref.shape[0], tf - head - valid), BF16)

    def conv(slot, r0, stage_c0, param_c0):
        u0_ref, u1_ref, u2_ref = slot
        rows = slice(r0, r0 + FF_EPI_ROWS)
        cols = slice(stage_c0, stage_c0 + V7X_LANES)
        pcols = slice(param_c0, param_c0 + V7X_LANES)
        sub = V7X_SUBLANES
        tap = [cw_ref[t * sub:(t + 1) * sub, pcols][None] for t in range(CONV_WIDTH)]
        piece = lambda ref: ref[rows, cols].reshape(FF_EPI_ROWS // sub, sub, V7X_LANES)
        c = cb_ref[:, pcols][None] + (tap[0] * piece(u2_ref) + tap[1] * piece(u1_ref) + tap[2] * piece(u0_ref))
        return c.reshape(FF_EPI_ROWS, V7X_LANES)

    def run(width):
        w = wbf_ref[...]
        if width < tf:
            w = jnp.concatenate([w[:, :width], w[:, tf:tf + width]], axis=1)
        for k in range(n_sub):
            s0 = k * sub_rows
            slot = slots[k % 2]
            u0_ref, u1_ref, u2_ref = slot
            _, prev_u1, prev_u2 = slots[(k + 1) % 2]
            u = jnp.dot(x_ref[s0:s0 + sub_rows, :w.shape[0]], w, preferred_element_type=F32)
            live = slice(0, 2 * width)
            u1_ref[0:hist, live] = prev_u1[sub_rows:, live]
            u2_ref[0:hist, live] = prev_u2[sub_rows:, live]
            u0_ref[:, live] = u
            u1_ref[1:1 + sub_rows, live] = u
            u2_ref[2:2 + sub_rows, live] = u
            for r0 in range(0, sub_rows, FF_EPI_ROWS):
                for c0 in range(0, width, V7X_LANES):
                    a = conv(slot, r0, c0, c0)
                    b = conv(slot, r0, width + c0, tf + c0)
                    g_ref[s0 + r0:s0 + r0 + FF_EPI_ROWS, c0:c0 + V7X_LANES] = (
                        a / (1.0 + jnp.exp(-a)) * b).astype(g_ref.dtype)
            if width < tf:
                g_ref[s0:s0 + sub_rows, width:] = jnp.zeros((sub_rows, tf - width), g_ref.dtype)

    @pl.when(j < FF_FULL_TILES)
    def _():
        run(tf)

    @pl.when(j == FF_FULL_TILES)
    def _():
        run(FF_REM)


def _ffn_conv_tables(conv_w, conv_b, *, tf=FF_TILE):
    nf, sub = FF_NUM_TILES, V7X_SUBLANES
    def per_tile(t):
        layers, taps = t.shape[:2]
        t = jnp.pad(t.reshape(layers, taps, 2, D_FF), ((0, 0), (0, 0), (0, 0), (0, D_FF_PAD - D_FF)))
        t = t.reshape(layers, taps, 2, nf, tf).transpose(0, 3, 1, 2, 4).reshape(layers, nf, taps, 1, 2 * tf)
        return jnp.broadcast_to(t, (layers, nf, taps, sub, 2 * tf)).reshape(layers, nf, taps * sub, 2 * tf)
    return per_tile(conv_w), per_tile(conv_b[:, None, :])


def _ffn_up(x, w_up, w_down, layer, cw, cb, *, tm=2048, tf=FF_TILE):
    m, pitch = x.shape
    d = w_up.shape[1]
    nf = FF_NUM_TILES
    head = tf - FF_REM
    assert tf % head == 0 and FF_REM % head == 0
    sub = V7X_SUBLANES
    grid = (nf, m // tm)
    wd_in, wd_out, wd_shape, wd_blocks = _side_cast_specs(w_down, layer, grid)
    blocks = ([((tm, pitch), BF16)] + [((d, tf), F32)] * 2 + [((d, head), F32)]
              + [((CONV_WIDTH * sub, 2 * tf), F32)] + [((sub, 2 * tf), F32)] + [((tm, tf), BF16)]
              + wd_blocks)
    slot = [((FF_SUB_ROWS, 2 * tf), F32)] + [((FF_SUB_ROWS + V7X_SUBLANES, 2 * tf), F32)] * 2
    scratch = [((d, 2 * tf), BF16)] + slot * 2
    return pl.pallas_call(
        _ffn_up_kernel,
        grid=grid,
        in_specs=[pl.BlockSpec((tm, pitch), lambda j, i: (i, 0)),
                  pl.BlockSpec((None, d, tf), lambda j, i: (layer, 0, j)),
                  pl.BlockSpec((None, d, head),
                               lambda j, i: (layer, 0, (FF_FULL_TILES + j) * (tf // head) + FF_REM // head)),
                  pl.BlockSpec((None, d, tf), lambda j, i: (layer, 0, FF_FULL_TILES + 1 + j)),
                  pl.BlockSpec((None, None, CONV_WIDTH * sub, 2 * tf), lambda j, i: (layer, j, 0, 0)),
                  pl.BlockSpec((None, None, sub, 2 * tf), lambda j, i: (layer, j, 0, 0)),
                  wd_in],
        out_specs=(pl.BlockSpec((tm, tf), lambda j, i: (i, j)), wd_out),
        out_shape=(jax.ShapeDtypeStruct((m, D_FF_PAD), BF16), wd_shape),
        scratch_shapes=[pltpu.VMEM(*shape_dtype) for shape_dtype in scratch],
        compiler_params=_params(2, _vmem_limit(blocks, scratch)),
        name="ffn_up",
    )(x, w_up, w_up, w_up, cw, cb, w_down)


SWA_BLOCKS_PER_STEP = 1


def _swa_kernel(slopes_ref, sinks_ref, q_ref, kp_ref, kc_ref, vp_ref, vc_ref, o_ref, bias_ref):
    n = pl.program_id(0)
    blk, dh, grp = ATT_BLOCK, ATT_HEAD_DIM, ATT_GROUP
    lanes = 2 * dh
    pairs = grp // 2

    qi = lax.broadcasted_iota(jnp.int32, (blk, blk), 0)
    kj = lax.broadcasted_iota(jnp.int32, (blk, blk), 1)
    from_prev = kj > qi

    def fill_bias(table, first_block):
        dist_f = jnp.where(from_prev, blk + qi - kj, qi - kj).astype(F32)
        assert WINDOW == blk
        for head in range(ATT_HEADS):
            bias = -(slopes_ref[head] * dist_f)
            bias_ref[table, head] = jnp.where(from_prev, -jnp.inf, bias) if first_block else bias

    @pl.when(n == 0)
    def _():
        fill_bias(0, True)
        for table in range(1, SWA_BLOCKS_PER_STEP):
            fill_bias(table, False)

    @pl.when(n == 1)
    def _():
        fill_bias(0, False)

    lane_kv = lax.broadcasted_iota(jnp.int32, (2 * blk, lanes), 1)
    lane_o = lax.broadcasted_iota(jnp.int32, (blk, lanes), 1)

    for b in range(SWA_BLOCKS_PER_STEP):
        rows = slice(b * blk, (b + 1) * blk)
        prev_rows = slice((b - 1) * blk, b * blk)
        for pair in range(ATT_KV_HEADS // 2):
            cols = slice(pair * lanes, (pair + 1) * lanes)
            k_prev = kp_ref[:, cols] if b == 0 else kc_ref[prev_rows, cols]
            v_prev = vp_ref[:, cols] if b == 0 else vc_ref[prev_rows, cols]
            k2 = jnp.concatenate([k_prev, kc_ref[rows, cols]], axis=0).astype(F32) * (dh ** -0.5)
            v2 = jnp.concatenate([v_prev, vc_ref[rows, cols]], axis=0).astype(F32)
            k2s = pltpu.roll(k2, dh, axis=1)
            v2s = pltpu.roll(v2, dh, axis=1)
            for sub in range(2):
                kvh = 2 * pair + sub
                k_lo, k_hi = (k2, k2s) if sub == 0 else (k2s, k2)
                v_lo, v_hi = (v2, v2s) if sub == 0 else (v2s, v2)
                kz = (jnp.where(lane_kv < dh, k_lo, 0.0).astype(BF16),
                      jnp.where(lane_kv >= dh, k_hi, 0.0).astype(BF16))
                vz = (jnp.where(lane_kv < dh, v_lo, 0.0).astype(BF16),
                      jnp.where(lane_kv >= dh, v_hi, 0.0).astype(BF16))
                qcol0 = kvh * grp * dh
                q_stack = jnp.concatenate(
                    [q_ref[rows, qcol0 + hp * lanes:qcol0 + (hp + 1) * lanes] for hp in range(pairs)], axis=0)
                acc = None
                inv = [[None, None] for _ in range(pairs)]
                for par in range(2):
                    s_all = lax.dot_general(q_stack, kz[par], (((1,), (1,)), ((), ())),
                                            preferred_element_type=F32)
                    e_parts = []
                    for hp in range(pairs):
                        head = kvh * grp + 2 * hp + par
                        s2 = s_all[hp * blk:(hp + 1) * blk]
                        s = jnp.where(from_prev, s2[:, :blk], s2[:, blk:]) + bias_ref[b, head]
                        sink = sinks_ref[head]
                        m = jnp.maximum(jnp.max(s, axis=-1, keepdims=True), sink)
                        e = jnp.exp(s - m)
                        denom = jnp.sum(e, axis=-1, keepdims=True) + jnp.exp(sink - m)
                        inv[hp][par] = 1.0 / denom
                        e = e.astype(BF16)
                        zero = jnp.zeros_like(e)
                        e_parts.append(jnp.concatenate(
                            [jnp.where(from_prev, e, zero), jnp.where(from_prev, zero, e)], axis=1))
                    pv = jnp.dot(jnp.concatenate(e_parts, axis=0), vz[par], preferred_element_type=F32)
                    acc = pv if acc is None else acc + pv
                for hp in range(pairs):
                    scale = jnp.where(lane_o < dh, inv[hp][0], inv[hp][1])
                    o_ref[rows, qcol0 + hp * lanes:qcol0 + (hp + 1) * lanes] = (
                        acc[hp * blk:(hp + 1) * blk] * scale).astype(o_ref.dtype)


def _swa(qkv, sinks):
    s = qkv.shape[0]
    blk = ATT_BLOCK
    dq = ATT_HEADS * ATT_HEAD_DIM
    dkv = ATT_KV_HEADS * ATT_HEAD_DIM
    k_blk = dq // dkv
    v_blk = k_blk + 1
    slopes = 2.0 ** (-8.0 * jnp.arange(1, ATT_HEADS + 1, dtype=F32) / ATT_HEADS)
    per_step = SWA_BLOCKS_PER_STEP
    rows = per_step * blk
    prev = lambda n: jnp.maximum(n * per_step - 1, 0)
    blocks = [((rows, dq), BF16)] * 2 + [((blk, dkv), BF16)] * 2 + [((rows, dkv), BF16)] * 2
    bias = ((per_step, ATT_HEADS, blk, blk), F32)
    return pl.pallas_call(
        _swa_kernel,
        grid=(s // rows,),
        in_specs=[pl.BlockSpec(memory_space=pltpu.SMEM),
                  pl.BlockSpec(memory_space=pltpu.SMEM),
                  pl.BlockSpec((rows, dq), lambda n: (n, 0)),
                  pl.BlockSpec((blk, dkv), lambda n: (prev(n), k_blk)),
                  pl.BlockSpec((rows, dkv), lambda n: (n, k_blk)),
                  pl.BlockSpec((blk, dkv), lambda n: (prev(n), v_blk)),
                  pl.BlockSpec((rows, dkv), lambda n: (n, v_blk))],
        out_specs=pl.BlockSpec((rows, dq), lambda n: (n, 0)),
        out_shape=jax.ShapeDtypeStruct((s, dq), BF16),
        scratch_shapes=[pltpu.VMEM(*bias)],
        compiler_params=_params(1, _vmem_limit(blocks, [bias])),
        name="swa",
    )(slopes, sinks.astype(F32), qkv, qkv, qkv, qkv, qkv)


def kernel(x, norm_mix_g, ret_w_in, ret_w_out, att_w_qkv, att_b_qkv, att_sinks, att_w_out,
           norm_ffn_g, ffn_w_up, ffn_conv_w, ffn_conv_b, ffn_w_down, final_norm_g):
    batch, seq, d = x.shape
    assert batch == 1 and d == D_MODEL
    depth = norm_mix_g.shape[0]

    h = x.reshape(seq, d)
    hn = _rmsnorm(h, norm_mix_g, 0)
    conv_taps, conv_bias = _ffn_conv_tables(ffn_conv_w, ffn_conv_b)
    for i in range(depth):
        j = i // N_MIXERS
        if i % N_MIXERS == 0:
            decay_table, chunk_decay = _ret_decay_tables()
            proj, w_out = _ret_in_proj(hn, ret_w_in, j, decay_table, ret_w_out)
            mixed = _retention(proj, chunk_decay)
        else:
            qkv, w_out = _matmul_bias(hn, att_w_qkv, att_b_qkv, j, att_w_out, tm=1024, tn=1280)
            mixed = _swa(qkv, att_sinks[j])
        h, hn = _proj_res_norm(mixed, w_out, h, norm_ffn_g, i, emit_residual=True)

        gated, w_down = _ffn_up(hn, ffn_w_up, ffn_w_down, i, conv_taps, conv_bias)
        if i + 1 < depth:
            h, hn = _proj_res_norm(gated, w_down, h, norm_mix_g, i + 1, emit_residual=True)
        else:
            out = _proj_res_norm(gated, w_down, h, final_norm_g, None, emit_residual=False)
    return out.reshape(batch, seq, d)
```

```python
import functools

import jax
import jax.numpy as jnp
import numpy as np
from jax import lax
from jax.experimental import pallas as pl
from jax.experimental.pallas import tpu as pltpu

F32 = jnp.float32
BF16 = jnp.bfloat16

D_MODEL = 2048
N_MIXERS = 2

RET_HEADS = 8
RET_QK_DIM = D_MODEL // RET_HEADS
RET_V_DIM = 2 * D_MODEL // RET_HEADS
RET_CHUNK = 256

ATT_HEAD_DIM = 64
ATT_HEADS = D_MODEL // ATT_HEAD_DIM
ATT_KV_HEADS = ATT_HEADS // 8
ATT_GROUP = ATT_HEADS // ATT_KV_HEADS
WINDOW = 128
ATT_BLOCK = 128

D_FF = ((8 * D_MODEL // 3 + 127) // 128) * 128
CONV_WIDTH = 3
EPS = 1e-6

V7X_LANES = 128
V7X_SUBLANES = 8
V7X_VMEM_BYTES = 64 * 1024 * 1024
V7X_COMPILER_SCRATCH_BYTES = 8 * 1024 * 1024
V7X_VMEM_MIN_RESERVE_BYTES = 52 * 1024 * 1024

FF_TILE = 512
D_FF_PAD = ((D_FF + FF_TILE - 1) // FF_TILE) * FF_TILE

D_MODEL_PITCH = D_MODEL + V7X_LANES


def _nbytes(shape, dtype):
    n = 1
    for s in shape:
        n *= s
    return n * jnp.dtype(dtype).itemsize


def _vmem_limit(pipelined, resident=()):
    total = 2 * sum(_nbytes(s, d) for s, d in pipelined)
    total += sum(_nbytes(s, d) for s, d in resident)
    total += V7X_COMPILER_SCRATCH_BYTES
    assert total <= V7X_VMEM_BYTES, total
    return max(total, V7X_VMEM_MIN_RESERVE_BYTES)


def _params(n_grid, vmem_limit):
    return pltpu.CompilerParams(
        dimension_semantics=("arbitrary",) * n_grid, vmem_limit_bytes=vmem_limit)


def _store_padded(ref, rows, value):
    n, d = value.shape
    ref[rows, :d] = value.astype(ref.dtype)
    ref[rows, d:] = jnp.zeros((n, ref.shape[1] - d), ref.dtype)


def _gain_operand(gain, layer):
    d = gain.shape[-1]
    if layer is None:
        return gain.reshape(1, d), pl.BlockSpec((1, d), lambda *_: (0, 0))
    return gain.reshape(gain.shape[0], 1, d), pl.BlockSpec((None, 1, d), lambda *_: (layer, 0, 0))


def _rmsnorm_kernel(x_ref, g_ref, o_ref):
    x = x_ref[...]
    ms = jnp.mean(x * x, axis=-1, keepdims=True)
    _store_padded(o_ref, slice(None), x * lax.rsqrt(ms + EPS) * g_ref[...])


def _rmsnorm(x, gain, layer, *, tm=1024):
    m, d = x.shape
    gain, gain_spec = _gain_operand(gain, layer)
    blocks = [((tm, d), F32), ((1, d), F32), ((tm, D_MODEL_PITCH), BF16)]
    return pl.pallas_call(
        _rmsnorm_kernel,
        grid=(m // tm,),
        in_specs=[pl.BlockSpec((tm, d), lambda i: (i, 0)),
                  gain_spec],
        out_specs=pl.BlockSpec((tm, D_MODEL_PITCH), lambda i: (i, 0)),
        out_shape=jax.ShapeDtypeStruct((m, D_MODEL_PITCH), BF16),
        compiler_params=_params(1, _vmem_limit(blocks)),
        name="rmsnorm",
    )(x, gain)


def _matmul_bias_kernel(x_ref, w_ref, b_ref, side_ref, o_ref, side_bf_ref, wbf_ref):
    side_bf_ref[...] = side_ref[...].astype(BF16)

    @pl.when(pl.program_id(1) == 0)
    def _():
        wbf_ref[...] = w_ref[...].astype(BF16)

    k = wbf_ref.shape[0]
    acc = jnp.dot(x_ref[:, :k], wbf_ref[...], preferred_element_type=F32) + b_ref[...]
    o_ref[...] = acc.astype(o_ref.dtype)


def _matmul_bias(x, w, bias, layer, side_w, *, tm, tn):
    m, pitch = x.shape
    k, n = w.shape[1:]
    grid = (n // tn, m // tm)
    side_in, side_out, side_shape, side_blocks = _side_cast_specs(side_w, layer, grid)
    blocks = [((tm, pitch), BF16), ((k, tn), F32), ((1, tn), F32), ((tm, tn), BF16)] + side_blocks
    return pl.pallas_call(
        _matmul_bias_kernel,
        grid=grid,
        in_specs=[pl.BlockSpec((tm, pitch), lambda j, i: (i, 0)),
                  pl.BlockSpec((None, k, tn), lambda j, i: (layer, 0, j)),
                  pl.BlockSpec((None, 1, tn), lambda j, i: (layer, 0, j)),
                  side_in],
        out_specs=(pl.BlockSpec((tm, tn), lambda j, i: (i, j)), side_out),
        out_shape=(jax.ShapeDtypeStruct((m, n), BF16), side_shape),
        scratch_shapes=[pltpu.VMEM((k, tn), BF16)],
        compiler_params=_params(2, _vmem_limit(blocks, [((k, tn), BF16)])),
        name="matmul_bias",
    )(x, w, bias.reshape(bias.shape[0], 1, n), side_w)


def _ret_decay_tables():
    hh, dk, cs = RET_HEADS, RET_QK_DIM, RET_CHUNK
    log_gamma = np.log1p(-(2.0 ** (-5.0 - np.arange(hh, dtype=np.float32)))).astype(np.float32)
    n = np.arange(cs, dtype=np.float32)[:, None]
    q_scale = np.exp(log_gamma[None, :] * (n + 1.0 - cs))
    k_scale = np.exp(log_gamma[None, :] * (cs - 1.0 - n)) * np.float32(dk ** -0.5)
    table = np.concatenate([np.repeat(q_scale, dk, axis=1), np.repeat(k_scale, dk, axis=1)], axis=1)
    chunk_decay = np.exp(log_gamma * np.float32(cs))
    return jnp.asarray(table, F32), jnp.asarray(chunk_decay, F32)


RET_PROJ_SUB_ROWS = 256


def _ret_in_proj_kernel(x_ref, w_ref, tab_ref, side_ref, o_ref, side_bf_ref, wbf_ref, *, n_scaled, n_plain):
    side_bf_ref[...] = side_ref[...].astype(BF16)
    j = pl.program_id(0)
    tm = x_ref.shape[0]
    cs = tab_ref.shape[0]

    @pl.when(pl.program_id(1) == 0)
    def _():
        wbf_ref[...] = w_ref[...].astype(BF16)

    def run(epilogue):
        w = wbf_ref[...]
        for r0 in range(0, tm, RET_PROJ_SUB_ROWS):
            acc = jnp.dot(x_ref[r0:r0 + RET_PROJ_SUB_ROWS, :w.shape[0]], w, preferred_element_type=F32)
            o_ref[r0:r0 + RET_PROJ_SUB_ROWS, :] = epilogue(acc, r0).astype(o_ref.dtype)

    def scaled(acc, r0):
        del r0
        rows, cols = acc.shape
        return (acc.reshape(rows // cs, cs, cols) * tab_ref[...][None]).reshape(rows, cols)

    @pl.when(j < n_scaled)
    def _():
        run(scaled)

    @pl.when((j >= n_scaled) & (j < n_scaled + n_plain))
    def _():
        run(lambda acc, r0: acc)

    @pl.when(j >= n_scaled + n_plain)
    def _():
        run(lambda acc, r0: acc / (1.0 + jnp.exp(-acc)))


def _ret_in_proj(x, w, layer, table, side_w, *, tm=2048, tn=1024):
    m, pitch = x.shape
    k, n = w.shape[1:]
    cs, scaled_cols = table.shape
    assert scaled_cols % tn == 0 and RET_PROJ_SUB_ROWS % cs == 0 and tm % RET_PROJ_SUB_ROWS == 0
    n_scaled = scaled_cols // tn
    n_plain = RET_HEADS * RET_V_DIM // tn
    grid = (n // tn, m // tm)
    side_in, side_out, side_shape, side_blocks = _side_cast_specs(side_w, layer, grid)
    blocks = [((tm, pitch), BF16), ((k, tn), F32), ((cs, tn), F32), ((tm, tn), BF16)] + side_blocks
    return pl.pallas_call(
        functools.partial(_ret_in_proj_kernel, n_scaled=n_scaled, n_plain=n_plain),
        grid=grid,
        in_specs=[pl.BlockSpec((tm, pitch), lambda j, i: (i, 0)),
                  pl.BlockSpec((None, k, tn), lambda j, i: (layer, 0, j)),
                  pl.BlockSpec((cs, tn), lambda j, i: (0, jnp.minimum(j, n_scaled - 1))),
                  side_in],
        out_specs=(pl.BlockSpec((tm, tn), lambda j, i: (i, j)), side_out),
        out_shape=(jax.ShapeDtypeStruct((m, n), BF16), side_shape),
        scratch_shapes=[pltpu.VMEM((k, tn), BF16)],
        compiler_params=_params(2, _vmem_limit(blocks, [((k, tn), BF16)])),
        name="ret_in_proj",
    )(x, w, table, side_w)


RET_CHUNKS_PER_STEP = 2


SIDE_CAST_ROWS = 128


def _side_cast_specs(w, layer, grid):
    k, d = w.shape[1:]
    n_chunks = k // SIDE_CAST_ROWS
    n_steps = 1
    for g in grid:
        n_steps *= g
    assert k % SIDE_CAST_ROWS == 0 and n_chunks <= n_steps

    def chunk_of(*idx):
        step = 0
        for g, i in zip(grid, idx):
            step = step * g + i
        return jnp.minimum(step, n_chunks - 1)

    in_spec = pl.BlockSpec((None, SIDE_CAST_ROWS, d), lambda *idx: (layer, chunk_of(*idx), 0))
    out_spec = pl.BlockSpec((SIDE_CAST_ROWS, d), lambda *idx: (chunk_of(*idx), 0))
    blocks = [((SIDE_CAST_ROWS, d), F32), ((SIDE_CAST_ROWS, d), BF16)]
    return in_spec, out_spec, jax.ShapeDtypeStruct((k, d), BF16), blocks


RET_RING_SLOTS = 3


def _retention_kernel(cd_ref, proj_ref, y_ref, q_buf, k_buf, v_buf, g_buf, sem, state_ref, sbf_ref):
    hh, dk, dv, cs = RET_HEADS, RET_QK_DIM, RET_V_DIM, RET_CHUNK
    c = pl.program_id(0)
    n_steps = pl.num_programs(0)
    step_rows = q_buf.shape[1]
    bufs = (q_buf, k_buf, v_buf, g_buf)

    def copies(step, slot):
        rows = pl.ds(pl.multiple_of(step * step_rows, step_rows), step_rows)
        col0, out = 0, []
        for i, buf in enumerate(bufs):
            width = buf.shape[2]
            out.append(pltpu.make_async_copy(proj_ref.at[rows, pl.ds(col0, width)], buf.at[slot],
                                             sem.at[slot, i]))
            col0 += width
        return out

    @pl.when(c == 0)
    def _():
        state_ref[...] = jnp.zeros_like(state_ref)
        sbf_ref[...] = jnp.zeros_like(sbf_ref)
        for cp in copies(0, 0):
            cp.start()

    @pl.when(jnp.logical_and(c == 0, n_steps > 1))
    def _():
        for cp in copies(1, 1):
            cp.start()

    @pl.when(c + 2 < n_steps)
    def _():
        for cp in copies(c + 2, lax.rem(c + 2, RET_RING_SLOTS)):
            cp.start()

    slot = lax.rem(c, RET_RING_SLOTS)
    for cp in copies(c, slot):
        cp.wait()
    q_ref, k_ref, v_ref, g_ref = (buf.at[slot] for buf in bufs)

    causal = (lax.broadcasted_iota(jnp.int32, (cs, cs), 0)
              >= lax.broadcasted_iota(jnp.int32, (cs, cs), 1))
    for r0 in range(0, step_rows, cs):
        rows = slice(r0, r0 + cs)
        for h in range(hh):
            q = q_ref[rows, h * dk:(h + 1) * dk]
            k = k_ref[rows, h * dk:(h + 1) * dk]
            v = v_ref[rows, h * dv:(h + 1) * dv]
            s = lax.dot_general(q, k, (((1,), (1,)), ((), ())), preferred_element_type=F32)
            p = jnp.where(causal, s, 0.0).astype(BF16)
            o = (jnp.dot(p, v, preferred_element_type=F32)
                 + jnp.dot(q, sbf_ref[h], preferred_element_type=F32))
            upd = lax.dot_general(k, v, (((0,), (0,)), ((), ())), preferred_element_type=F32)
            new_state = (state_ref[h] + upd) * cd_ref[h]
            state_ref[h] = new_state
            sbf_ref[h] = new_state.astype(BF16)

            ms = jnp.mean(o * o, axis=-1, keepdims=True)
            gate = g_ref[rows, h * dv:(h + 1) * dv].astype(F32)
            y_ref[rows, h * dv:(h + 1) * dv] = (gate * (o * lax.rsqrt(ms + EPS))).astype(y_ref.dtype)


def _retention(proj, chunk_decay):
    s = proj.shape[0]
    hh, dk, dv, cs = RET_HEADS, RET_QK_DIM, RET_V_DIM, RET_CHUNK
    qk_w, v_w = hh * dk, hh * dv
    assert 2 * qk_w == v_w
    rows = RET_CHUNKS_PER_STEP * cs
    assert proj.shape[1] == 2 * qk_w + 2 * v_w and s % rows == 0
    slots = RET_RING_SLOTS
    ring = [((slots, rows, qk_w), BF16)] * 2 + [((slots, rows, v_w), BF16)] * 2
    scratch = ring + [((hh, dk, dv), F32), ((hh, dk, dv), BF16)]
    return pl.pallas_call(
        _retention_kernel,
        grid=(s // rows,),
        in_specs=[pl.BlockSpec(memory_space=pltpu.SMEM),
                  pl.BlockSpec(memory_space=pl.ANY)],
        out_specs=pl.BlockSpec((rows, v_w), lambda c: (c, 0)),
        out_shape=jax.ShapeDtypeStruct((s, v_w), BF16),
        scratch_shapes=([pltpu.VMEM(*shape_dtype) for shape_dtype in ring]
                        + [pltpu.SemaphoreType.DMA((slots, len(ring)))]
                        + [pltpu.VMEM((hh, dk, dv), F32), pltpu.VMEM((hh, dk, dv), BF16)]),
        compiler_params=_params(1, _vmem_limit([((rows, v_w), BF16)], scratch)),
        name="retention",
    )(chunk_decay, proj)


def _proj_res_norm_kernel(a_ref, w_ref, res_ref, gain_ref, *out_refs, emit_residual):
    h = res_ref[...] + jnp.dot(a_ref[...], w_ref[...], preferred_element_type=F32)
    ms = jnp.mean(h * h, axis=-1, keepdims=True)
    hn = h * lax.rsqrt(ms + EPS) * gain_ref[...]
    if emit_residual:
        h_ref, hn_ref = out_refs
        h_ref[...] = h
        _store_padded(hn_ref, slice(None), hn)
    else:
        (hn_ref,) = out_refs
        hn_ref[...] = hn.astype(hn_ref.dtype)


def _proj_res_norm(a, w, res, gain, layer, *, emit_residual, tm=512):
    gain, gain_spec = _gain_operand(gain, layer)
    m = a.shape[0]
    k, d = w.shape
    row = lambda s: (s, 0)
    if emit_residual:
        out_shape = (jax.ShapeDtypeStruct((m, d), F32), jax.ShapeDtypeStruct((m, D_MODEL_PITCH), BF16))
        out_specs = (pl.BlockSpec((tm, d), row), pl.BlockSpec((tm, D_MODEL_PITCH), row))
        out_blocks = [((tm, d), F32), ((tm, D_MODEL_PITCH), BF16)]
    else:
        out_shape = jax.ShapeDtypeStruct((m, d), F32)
        out_specs = pl.BlockSpec((tm, d), row)
        out_blocks = [((tm, d), F32)]
    blocks = [((tm, k), BF16), ((tm, d), F32), ((1, d), F32)] + out_blocks
    return pl.pallas_call(
        functools.partial(_proj_res_norm_kernel, emit_residual=emit_residual),
        grid=(m // tm,),
        in_specs=[pl.BlockSpec((tm, k), row),
                  pl.BlockSpec((k, d), lambda s: (0, 0), pipeline_mode=pl.Buffered(1)),
                  pl.BlockSpec((tm, d), row),
                  gain_spec],
        out_specs=out_specs,
        out_shape=out_shape,
        compiler_params=_params(1, _vmem_limit(blocks, [((k, d), BF16)])),
        name="proj_res_norm" if emit_residual else "proj_res_final_norm",
    )(a, w, res, gain)


FF_FULL_TILES = D_FF // FF_TILE
FF_REM = D_FF % FF_TILE
FF_NUM_TILES = FF_FULL_TILES + 1
FF_EPI_ROWS = 32


FF_SUB_ROWS = 256


def _ffn_up_kernel(x_ref, wa_ref, wb0_ref, wb1_ref, cw_ref, cb_ref, wd_ref, g_ref, wd_bf_ref, wbf_ref,
                   *stage_refs):
    wd_bf_ref[...] = wd_ref[...].astype(BF16)
    j = pl.program_id(0)
    i = pl.program_id(1)
    tm = x_ref.shape[0]
    tf = wa_ref.shape[1]
    head = wb0_ref.shape[1]
    hist = V7X_SUBLANES
    sub_rows = FF_SUB_ROWS
    n_sub = tm // sub_rows
    assert n_sub % 2 == 0
    slots = (stage_refs[0:3], stage_refs[3:6])
    last_u1, last_u2 = slots[1][1], slots[1][2]

    @pl.when((i == 0) & (j == 0))
    def _():
        for _, u1_ref, u2_ref in slots:
            u1_ref[sub_rows:, :] = jnp.zeros((hist, 2 * tf), F32)
            u2_ref[sub_rows:, :] = jnp.zeros((hist, 2 * tf), F32)

    @pl.when(i == 0)
    def _():
        wbf_ref[:, :tf] = wa_ref[...].astype(BF16)
        wbf_ref[:, tf:tf + head] = wb0_ref[...].astype(BF16)
        last_u1[sub_rows:, :] = jnp.zeros((hist, 2 * tf), F32)
        last_u2[sub_rows:, :] = jnp.zeros((hist, 2 * tf), F32)

    @pl.when((i == 0) & (j < FF_FULL_TILES))
    def _():
        wbf_ref[:, tf + head:] = wb1_ref[:, :FF_REM].astype(BF16)

    @pl.when((i == 0) & (j == FF_FULL_TILES))
    def _():
        valid = FF_REM - head
        wbf_ref[:, tf + head:tf + head + valid] = wb1_ref[:, :valid].astype(BF16)
        wbf_ref[:, tf + head + valid:] = jnp.zeros((wbf_ref.shape[0], tf - head - valid), BF16)

    def conv(slot, r0, stage_c0, param_c0):
        u0_ref, u1_ref, u2_ref = slot
        rows = slice(r0, r0 + FF_EPI_ROWS)
        cols = slice(stage_c0, stage_c0 + V7X_LANES)
        pcols = slice(param_c0, param_c0 + V7X_LANES)
        sub = V7X_SUBLANES
        tap = [cw_ref[t * sub:(t + 1) * sub, pcols][None] for t in range(CONV_WIDTH)]
        piece = lambda ref: ref[rows, cols].reshape(FF_EPI_ROWS // sub, sub, V7X_LANES)
        c = cb_ref[:, pcols][None] + (tap[0] * piece(u2_ref) + tap[1] * piece(u1_ref) + tap[2] * piece(u0_ref))
        return c.reshape(FF_EPI_ROWS, V7X_LANES)

    def run(width):
        w = wbf_ref[...]
        if width < tf:
            w = jnp.concatenate([w[:, :width], w[:, tf:tf + width]], axis=1)
        for k in range(n_sub):
            s0 = k * sub_rows
            slot = slots[k % 2]
            u0_ref, u1_ref, u2_ref = slot
            _, prev_u1, prev_u2 = slots[(k + 1) % 2]
            u = jnp.dot(x_ref[s0:s0 + sub_rows, :w.shape[0]], w, preferred_element_type=F32)
            live = slice(0, 2 * width)
            u1_ref[0:hist, live] = prev_u1[sub_rows:, live]
            u2_ref[0:hist, live] = prev_u2[sub_rows:, live]
            u0_ref[:, live] = u
            u1_ref[1:1 + sub_rows, live] = u
            u2_ref[2:2 + sub_rows, live] = u
            for r0 in range(0, sub_rows, FF_EPI_ROWS):
                for c0 in range(0, width, V7X_LANES):
                    a = conv(slot, r0, c0, c0)
                    b = conv(slot, r0, width + c0, tf + c0)
                    g_ref[s0 + r0:s0 + r0 + FF_EPI_ROWS, c0:c0 + V7X_LANES] = (
                        a / (1.0 + jnp.exp(-a)) * b).astype(g_ref.dtype)
            if width < tf:
                g_ref[s0:s0 + sub_rows, width:] = jnp.zeros((sub_rows, tf - width), g_ref.dtype)

    @pl.when(j < FF_FULL_TILES)
    def _():
        run(tf)

    @pl.when(j == FF_FULL_TILES)
    def _():
        run(FF_REM)


def _ffn_conv_tables(conv_w, conv_b, *, tf=FF_TILE):
    nf, sub = FF_NUM_TILES, V7X_SUBLANES
    def per_tile(t):
        layers, taps = t.shape[:2]
        t = jnp.pad(t.reshape(layers, taps, 2, D_FF), ((0, 0), (0, 0), (0, 0), (0, D_FF_PAD - D_FF)))
        t = t.reshape(layers, taps, 2, nf, tf).transpose(0, 3, 1, 2, 4).reshape(layers, nf, taps, 1, 2 * tf)
        return jnp.broadcast_to(t, (layers, nf, taps, sub, 2 * tf)).reshape(layers, nf, taps * sub, 2 * tf)
    return per_tile(conv_w), per_tile(conv_b[:, None, :])


def _ffn_up(x, w_up, w_down, layer, cw, cb, *, tm=2048, tf=FF_TILE):
    m, pitch = x.shape
    d = w_up.shape[1]
    nf = FF_NUM_TILES
    head = tf - FF_REM
    assert tf % head == 0 and FF_REM % head == 0
    sub = V7X_SUBLANES
    grid = (nf, m // tm)
    wd_in, wd_out, wd_shape, wd_blocks = _side_cast_specs(w_down, layer, grid)
    blocks = ([((tm, pitch), BF16)] + [((d, tf), F32)] * 2 + [((d, head), F32)]
              + [((CONV_WIDTH * sub, 2 * tf), F32)] + [((sub, 2 * tf), F32)] + [((tm, tf), BF16)]
              + wd_blocks)
    slot = [((FF_SUB_ROWS, 2 * tf), F32)] + [((FF_SUB_ROWS + V7X_SUBLANES, 2 * tf), F32)] * 2
    scratch = [((d, 2 * tf), BF16)] + slot * 2
    return pl.pallas_call(
        _ffn_up_kernel,
        grid=grid,
        in_specs=[pl.BlockSpec((tm, pitch), lambda j, i: (i, 0)),
                  pl.BlockSpec((None, d, tf), lambda j, i: (layer, 0, j)),
                  pl.BlockSpec((None, d, head),
                               lambda j, i: (layer, 0, (FF_FULL_TILES + j) * (tf // head) + FF_REM // head)),
                  pl.BlockSpec((None, d, tf), lambda j, i: (layer, 0, FF_FULL_TILES + 1 + j)),
                  pl.BlockSpec((None, None, CONV_WIDTH * sub, 2 * tf), lambda j, i: (layer, j, 0, 0)),
                  pl.BlockSpec((None, None, sub, 2 * tf), lambda j, i: (layer, j, 0, 0)),
                  wd_in],
        out_specs=(pl.BlockSpec((tm, tf), lambda j, i: (i, j)), wd_out),
        out_shape=(jax.ShapeDtypeStruct((m, D_FF_PAD), BF16), wd_shape),
        scratch_shapes=[pltpu.VMEM(*shape_dtype) for shape_dtype in scratch],
        compiler_params=_params(2, _vmem_limit(blocks, scratch)),
        name="ffn_up",
    )(x, w_up, w_up, w_up, cw, cb, w_down)


SWA_BLOCKS_PER_STEP = 1


def _swa_kernel(slopes_ref, sinks_ref, q_ref, kp_ref, kc_ref, vp_ref, vc_ref, o_ref, bias_ref):
    n = pl.program_id(0)
    blk, dh, grp = ATT_BLOCK, ATT_HEAD_DIM, ATT_GROUP
    lanes = 2 * dh
    pairs = grp // 2

    qi = lax.broadcasted_iota(jnp.int32, (blk, blk), 0)
    kj = lax.broadcasted_iota(jnp.int32, (blk, blk), 1)
    from_prev = kj > qi

    def fill_bias(table, first_block):
        dist_f = jnp.where(from_prev, blk + qi - kj, qi - kj).astype(F32)
        assert WINDOW == blk
        for head in range(ATT_HEADS):
            bias = -(slopes_ref[head] * dist_f)
            bias_ref[table, head] = jnp.where(from_prev, -jnp.inf, bias) if first_block else bias

    @pl.when(n == 0)
    def _():
        fill_bias(0, True)
        for table in range(1, SWA_BLOCKS_PER_STEP):
            fill_bias(table, False)

    @pl.when(n == 1)
    def _():
        fill_bias(0, False)

    lane_kv = lax.broadcasted_iota(jnp.int32, (2 * blk, lanes), 1)
    lane_o = lax.broadcasted_iota(jnp.int32, (blk, lanes), 1)

    for b in range(SWA_BLOCKS_PER_STEP):
        rows = slice(b * blk, (b + 1) * blk)
        prev_rows = slice((b - 1) * blk, b * blk)
        for pair in range(ATT_KV_HEADS // 2):
            cols = slice(pair * lanes, (pair + 1) * lanes)
            k_prev = kp_ref[:, cols] if b == 0 else kc_ref[prev_rows, cols]
            v_prev = vp_ref[:, cols] if b == 0 else vc_ref[prev_rows, cols]
            k2 = jnp.concatenate([k_prev, kc_ref[rows, cols]], axis=0).astype(F32) * (dh ** -0.5)
            v2 = jnp.concatenate([v_prev, vc_ref[rows, cols]], axis=0).astype(F32)
            k2s = pltpu.roll(k2, dh, axis=1)
            v2s = pltpu.roll(v2, dh, axis=1)
            for sub in range(2):
                kvh = 2 * pair + sub
                k_lo, k_hi = (k2, k2s) if sub == 0 else (k2s, k2)
                v_lo, v_hi = (v2, v2s) if sub == 0 else (v2s, v2)
                kz = (jnp.where(lane_kv < dh, k_lo, 0.0).astype(BF16),
                      jnp.where(lane_kv >= dh, k_hi, 0.0).astype(BF16))
                vz = (jnp.where(lane_kv < dh, v_lo, 0.0).astype(BF16),
                      jnp.where(lane_kv >= dh, v_hi, 0.0).astype(BF16))
                qcol0 = kvh * grp * dh
                q_stack = jnp.concatenate(
                    [q_ref[rows, qcol0 + hp * lanes:qcol0 + (hp + 1) * lanes] for hp in range(pairs)], axis=0)
                acc = None
                inv = [[None, None] for _ in range(pairs)]
                for par in range(2):
                    s_all = lax.dot_general(q_stack, kz[par], (((1,), (1,)), ((), ())),
                                            preferred_element_type=F32)
                    e_parts = []
                    for hp in range(pairs):
                        head = kvh * grp + 2 * hp + par
                        s2 = s_all[hp * blk:(hp + 1) * blk]
                        s = jnp.where(from_prev, s2[:, :blk], s2[:, blk:]) + bias_ref[b, head]
                        sink = sinks_ref[head]
                        m = jnp.maximum(jnp.max(s, axis=-1, keepdims=True), sink)
                        e = jnp.exp(s - m)
                        denom = jnp.sum(e, axis=-1, keepdims=True) + jnp.exp(sink - m)
                        inv[hp][par] = 1.0 / denom
                        e = e.astype(BF16)
                        zero = jnp.zeros_like(e)
                        e_parts.append(jnp.concatenate(
                            [jnp.where(from_prev, e, zero), jnp.where(from_prev, zero, e)], axis=1))
                    pv = jnp.dot(jnp.concatenate(e_parts, axis=0), vz[par], preferred_element_type=F32)
                    acc = pv if acc is None else acc + pv
                for hp in range(pairs):
                    scale = jnp.where(lane_o < dh, inv[hp][0], inv[hp][1])
                    o_ref[rows, qcol0 + hp * lanes:qcol0 + (hp + 1) * lanes] = (
                        acc[hp * blk:(hp + 1) * blk] * scale).astype(o_ref.dtype)


def _swa(qkv, sinks):
    s = qkv.shape[0]
    blk = ATT_BLOCK
    dq = ATT_HEADS * ATT_HEAD_DIM
    dkv = ATT_KV_HEADS * ATT_HEAD_DIM
    k_blk = dq // dkv
    v_blk = k_blk + 1
    slopes = 2.0 ** (-8.0 * jnp.arange(1, ATT_HEADS + 1, dtype=F32) / ATT_HEADS)
    per_step = SWA_BLOCKS_PER_STEP
    rows = per_step * blk
    prev = lambda n: jnp.maximum(n * per_step - 1, 0)
    blocks = [((rows, dq), BF16)] * 2 + [((blk, dkv), BF16)] * 2 + [((rows, dkv), BF16)] * 2
    bias = ((per_step, ATT_HEADS, blk, blk), F32)
    return pl.pallas_call(
        _swa_kernel,
        grid=(s // rows,),
        in_specs=[pl.BlockSpec(memory_space=pltpu.SMEM),
                  pl.BlockSpec(memory_space=pltpu.SMEM),
                  pl.BlockSpec((rows, dq), lambda n: (n, 0)),
                  pl.BlockSpec((blk, dkv), lambda n: (prev(n), k_blk)),
                  pl.BlockSpec((rows, dkv), lambda n: (n, k_blk)),
                  pl.BlockSpec((blk, dkv), lambda n: (prev(n), v_blk)),
                  pl.BlockSpec((rows, dkv), lambda n: (n, v_blk))],
        out_specs=pl.BlockSpec((rows, dq), lambda n: (n, 0)),
        out_shape=jax.ShapeDtypeStruct((s, dq), BF16),
        scratch_shapes=[pltpu.VMEM(*bias)],
        compiler_params=_params(1, _vmem_limit(blocks, [bias])),
        name="swa",
    )(slopes, sinks.astype(F32), qkv, qkv, qkv, qkv, qkv)


def kernel(x, norm_mix_g, ret_w_in, ret_w_out, att_w_qkv, att_b_qkv, att_sinks, att_w_out,
           norm_ffn_g, ffn_w_up, ffn_conv_w, ffn_conv_b, ffn_w_down, final_norm_g):
    batch, seq, d = x.shape
    assert batch == 1 and d == D_MODEL
    depth = norm_mix_g.shape[0]

    h = x.reshape(seq, d)
    hn = _rmsnorm(h, norm_mix_g, 0)
    conv_taps, conv_bias = _ffn_conv_tables(ffn_conv_w, ffn_conv_b)
    for i in range(depth):
        j = i // N_MIXERS
        if i % N_MIXERS == 0:
            decay_table, chunk_decay = _ret_decay_tables()
            proj, w_out = _ret_in_proj(hn, ret_w_in, j, decay_table, ret_w_out)
            mixed = _retention(proj, chunk_decay)
        else:
            qkv, w_out = _matmul_bias(hn, att_w_qkv, att_b_qkv, j, att_w_out, tm=1024, tn=1280)
            mixed = _swa(qkv, att_sinks[j])
        h, hn = _proj_res_norm(mixed, w_out, h, norm_ffn_g, i, emit_residual=True)

        gated, w_down = _ffn_up(hn, ffn_w_up, ffn_w_down, i, conv_taps, conv_bias)
        if i + 1 < depth:
            h, hn = _proj_res_norm(gated, w_down, h, norm_mix_g, i + 1, emit_residual=True)
        else:
            out = _proj_res_norm(gated, w_down, h, final_norm_g, None, emit_residual=False)
    return out.reshape(batch, seq, d)
```
